```python
import jax
import jax.numpy as jnp
from jax import lax
import numpy as np

D_MODEL = 4096
BATCH = 4
SEQ = 2048
DEPTH = 2
DEC_BATCH = 8
DEC_SEQ = 8
PAST_LEN = 16384
PAGE_SIZE = 128

N_A_LAYERS = DEPTH // 2
N_B_LAYERS = DEPTH - N_A_LAYERS
MIX_W = D_MODEL
MEM_TOKENS = 256
MEM_HEADS = 4
MEM_W = MIX_W // 4
MEM_HEAD_DIM = MEM_W // MEM_HEADS
TOK_W = MIX_W - MEM_W
POOL_WINDOWS = (2, 4, 8, 16)
POOL_GROUP = TOK_W // len(POOL_WINDOWS)
POOL_BUF = max(POOL_WINDOWS) - 1
HEAD_DIM = 128
N_HEADS = TOK_W // HEAD_DIM
N_KV_HEADS = 8
KV_GROUP = N_HEADS // N_KV_HEADS
KV_W = N_KV_HEADS * HEAD_DIM
MOBA_BLOCK = 256
MOBA_TOPK = 3
Q_CHUNK = 8
N_GROUPS = 4
EXPERTS_PER_GROUP = 4
N_EXPERTS = N_GROUPS * EXPERTS_PER_GROUP
EXPERT_TOPK = 2
D_EXPERT = D_MODEL // 4
EPS = 1e-6

kernel_name = 'moba_pool_yoco_hmoe_decode_step'


def rms_norm(x, g):
    xf = x.astype(jnp.float32)
    y = xf * lax.rsqrt(jnp.mean(xf * xf, axis=-1, keepdims=True) + EPS)
    return (y * g.astype(jnp.float32)).astype(x.dtype)


def alibi_slopes(n):
    return jnp.exp2(-8.0 * jnp.arange(1, n + 1, dtype=jnp.float32) / n)


def pool_mix(u, buf, pos, w_pool, scale):
    t = u.shape[1]
    z = jnp.concatenate([buf, u], axis=1).astype(jnp.float32)
    csum = jnp.concatenate([jnp.zeros_like(z[:, :1]), jnp.cumsum(z, axis=1)], axis=1)
    cur = z[:, POOL_BUF:]
    outs = []
    for gi, win in enumerate(POOL_WINDOWS):
        sl = slice(gi * POOL_GROUP, (gi + 1) * POOL_GROUP)
        wsum = csum[:, POOL_BUF + 1:POOL_BUF + 1 + t, sl] - csum[:, POOL_BUF + 1 - win:POOL_BUF + 1 - win + t, sl]
        cnt = jnp.minimum(pos + 1, win).astype(jnp.float32)[None, :, None]
        d = (wsum / cnt - cur[..., sl]).astype(u.dtype)
        outs.append(jnp.einsum('btc,cd->btd', d, w_pool[gi]))
    y = jnp.concatenate(outs, axis=-1) * scale
    return y, z[:, -POOL_BUF:].astype(u.dtype)


def mem_kv(mem, g_norm, w_kv, g_k):
    b, m, _ = mem.shape
    kv = jnp.einsum('bmd,de->bme', rms_norm(mem, g_norm), w_kv)
    k = rms_norm(kv[..., :MEM_W].reshape(b, m, MEM_HEADS, MEM_HEAD_DIM), g_k)
    v = kv[..., MEM_W:].reshape(b, m, MEM_HEADS, MEM_HEAD_DIM)
    return k, v


def mem_attend(q, k, v):
    b, t = q.shape[:2]
    s = jnp.einsum('bthd,bmhd->bhtm', q, k).astype(jnp.float32) * (MEM_HEAD_DIM ** -0.5)
    p = jax.nn.softmax(s, axis=-1).astype(v.dtype)
    return jnp.einsum('bhtm,bmhd->bthd', p, v).reshape(b, t, MEM_W)


def shared_kv(x, g_norm, w_kv, g_k):
    b, t, _ = x.shape
    kv = jnp.einsum('btd,de->bte', rms_norm(x, g_norm), w_kv)
    k = rms_norm(kv[..., :KV_W].reshape(b, t, N_KV_HEADS, HEAD_DIM), g_k)
    v = kv[..., KV_W:].reshape(b, t, N_KV_HEADS, HEAD_DIM)
    return k, v


def kv_blocks(k, v):
    b, length = k.shape[:2]
    nb = -(-length // MOBA_BLOCK)
    pad = nb * MOBA_BLOCK - length

    def blk(a):
        a = jnp.pad(a, ((0, 0), (0, pad), (0, 0), (0, 0)))
        return a.reshape(b, nb, MOBA_BLOCK, N_KV_HEADS, HEAD_DIM).transpose(0, 3, 1, 2, 4)

    kb = blk(k)
    vb = blk(v)
    kmean = jnp.mean(kb.astype(jnp.float32), axis=3).astype(k.dtype)
    return kb, vb, kmean


def moba_attend(q, pos, kb, vb, kmean, slopes):
    b, t = q.shape[:2]
    nb = kb.shape[2]
    topk = min(MOBA_TOPK, nb)
    qg = q.reshape(b, t, N_KV_HEADS, KV_GROUP, HEAD_DIM).transpose(0, 2, 3, 1, 4)
    own = pos // MOBA_BLOCK
    gate = jnp.einsum('bgrtd,bgnd->bgrtn', qg, kmean).astype(jnp.float32)
    past = jnp.arange(nb)[None, :] < own[:, None]
    gate = jnp.where(past, gate, -jnp.inf)
    _, sel = lax.top_k(gate, topk)
    sel_ok = sel < own[:, None]
    idx = jnp.concatenate([sel, jnp.broadcast_to(own[:, None], sel.shape[:-1] + (1,))], axis=-1)
    ok = jnp.concatenate([sel_ok, jnp.ones(sel.shape[:-1] + (1,), bool)], axis=-1)
    bi = jnp.arange(b)[:, None, None, None, None]
    gi = jnp.arange(N_KV_HEADS)[None, :, None, None, None]
    kg = kb[bi, gi, idx]
    vg = vb[bi, gi, idx]
    kpos = idx[..., None] * MOBA_BLOCK + jnp.arange(MOBA_BLOCK)
    s = jnp.einsum('bgrtd,bgrtjsd->bgrtjs', qg, kg).astype(jnp.float32) * (HEAD_DIM ** -0.5)
    dist = (pos[:, None, None] - kpos).astype(jnp.float32)
    s = s - slopes.reshape(N_KV_HEADS, KV_GROUP)[None, :, :, None, None, None] * dist
    mask = ok[..., None] & (kpos <= pos[:, None, None])
    s = jnp.where(mask, s, -jnp.inf)
    p = jax.nn.softmax(s, axis=(-2, -1)).astype(vg.dtype)
    o = jnp.einsum('bgrtjs,bgrtjsd->bgrtd', p, vg)
    return o.transpose(0, 3, 1, 2, 4).reshape(b, t, N_HEADS * HEAD_DIM)


def moba(q, pos, kb, vb, kmean, slopes):
    b, t = q.shape[:2]
    c = Q_CHUNK if t % Q_CHUNK == 0 else t
    n = t // c
    qc = q.reshape(b, n, c, N_HEADS, HEAD_DIM).transpose(1, 0, 2, 3, 4)
    out = lax.map(lambda a: moba_attend(a[0], a[1], kb, vb, kmean, slopes), (qc, pos.reshape(n, c)))
    return out.transpose(1, 0, 2, 3).reshape(b, t, N_HEADS * HEAD_DIM)


def hier_moe(h, w_group, b_group, w_router, b_router, w_gate_up, w_down):
    n = h.shape[0]
    g_logits = jnp.einsum('nd,dg->ng', h, w_group).astype(jnp.float32) + b_group.astype(jnp.float32)
    g_sel = jnp.argmax(g_logits, axis=-1)
    p_group = jnp.take_along_axis(jax.nn.softmax(g_logits, axis=-1), g_sel[:, None], axis=-1)
    e_logits = jnp.einsum('nd,de->ne', h, w_router).astype(jnp.float32) + b_router.astype(jnp.float32)
    e_logits = jnp.take_along_axis(e_logits.reshape(n, N_GROUPS, EXPERTS_PER_GROUP), g_sel[:, None, None], axis=1)[:, 0]
    e_val, e_idx = lax.top_k(e_logits, EXPERT_TOPK)
    e_w = jax.nn.softmax(e_val, axis=-1) * p_group
    e_id = g_sel[:, None] * EXPERTS_PER_GROUP + e_idx
    combine = jnp.einsum('nk,nke->ne', e_w, jax.nn.one_hot(e_id, N_EXPERTS, dtype=jnp.float32)).astype(h.dtype)
    y = jnp.zeros_like(h)
    for e in range(N_EXPERTS):
        gu = jnp.einsum('nd,df->nf', h, w_gate_up[e])
        act = jax.nn.silu(gu[:, :D_EXPERT]) * gu[:, D_EXPERT:]
        y = y + combine[:, e:e + 1] * jnp.einsum('nf,fd->nd', act, w_down[e])
    return y


def _trunk(x, mem_k, mem_v, pool_buf, start, past_k, past_v, w):
    b, t, d = x.shape
    pos = start + jnp.arange(t)
    slopes = alibi_slopes(N_HEADS)
    new_pool = []
    new_k = None
    new_v = None
    blocks = None
    for l in range(DEPTH):
        z = jnp.einsum('btd,de->bte', rms_norm(x, w['g_mix_norm'][l]), w['w_in'][l])
        mq = rms_norm(z[..., TOK_W:].reshape(b, t, MEM_HEADS, MEM_HEAD_DIM), w['g_mem_q'][l])
        mem_y = mem_attend(mq, mem_k[l], mem_v[l])
        if l < N_A_LAYERS:
            tok_y, buf = pool_mix(z[..., :TOK_W], pool_buf[l], pos, w['w_pool'][l], w['pool_scale'][l])
            new_pool.append(buf)
        else:
            if blocks is None:
                new_k, new_v = shared_kv(x, w['g_kv_norm'], w['w_kv'], w['g_k'])
                k_all = new_k if past_k is None else jnp.concatenate([past_k, new_k], axis=1)
                v_all = new_v if past_v is None else jnp.concatenate([past_v, new_v], axis=1)
                blocks = kv_blocks(k_all, v_all)
            q = rms_norm(z[..., :TOK_W].reshape(b, t, N_HEADS, HEAD_DIM), w['g_q'][l - N_A_LAYERS])
            tok_y = moba(q, pos, blocks[0], blocks[1], blocks[2], slopes)
        x = x + jnp.einsum('bte,ed->btd', jnp.concatenate([tok_y, mem_y], axis=-1), w['w_out'][l])
        h = rms_norm(x, w['g_ffn_norm'][l]).reshape(b * t, d)
        x = x + hier_moe(h, w['w_group'][l], w['b_group'][l], w['w_router'][l], w['b_router'][l],
                         w['w_gate_up'][l], w['w_down'][l]).reshape(b, t, d)
    return x, jnp.stack(new_pool), new_k, new_v


def setup_inputs(seed: int = 0) -> dict:
    key = jax.random.key(seed)
    ks = jax.random.split(key, 32)

    def nrm(i, shape, scale=1.0):
        return jax.random.normal(ks[i], shape, jnp.float32) * scale

    def gain(i, shape):
        return 1.0 + 0.05 * jax.random.normal(ks[i], shape, jnp.float32)

    n_pages = PAST_LEN // PAGE_SIZE
    n_pool_pages = (5 * DEC_BATCH * n_pages + 3) // 4
    perm = jax.random.permutation(ks[0], n_pool_pages)
    page_table = perm[:DEC_BATCH * n_pages].reshape(DEC_BATCH, n_pages).astype(jnp.int32)
    return {
        'x_prompt': nrm(1, (BATCH, SEQ, D_MODEL)),
        'x_sample': nrm(2, (DEC_BATCH, DEC_SEQ, D_MODEL)),
        'mem_prompt': nrm(3, (BATCH, MEM_TOKENS, D_MODEL)),
        'cache_k': nrm(4, (n_pool_pages, PAGE_SIZE, N_KV_HEADS, HEAD_DIM)),
        'cache_v': nrm(5, (n_pool_pages, PAGE_SIZE, N_KV_HEADS, HEAD_DIM)),
        'cache_mem_k': nrm(6, (DEPTH, DEC_BATCH, MEM_TOKENS, MEM_HEADS, MEM_HEAD_DIM)),
        'cache_mem_v': nrm(7, (DEPTH, DEC_BATCH, MEM_TOKENS, MEM_HEADS, MEM_HEAD_DIM)),
        'state_pool': nrm(8, (N_A_LAYERS, DEC_BATCH, POOL_BUF, TOK_W)),
        'page_table': page_table,
        'g_mix_norm': gain(9, (DEPTH, D_MODEL)),
        'w_in': nrm(10, (DEPTH, D_MODEL, MIX_W), D_MODEL ** -0.5),
        'w_out': nrm(11, (DEPTH, MIX_W, D_MODEL), MIX_W ** -0.5),
        'w_pool': nrm(12, (N_A_LAYERS, len(POOL_WINDOWS), POOL_GROUP, POOL_GROUP), POOL_GROUP ** -0.5),
        'pool_scale': gain(13, (N_A_LAYERS, TOK_W)),
        'g_q': gain(14, (N_B_LAYERS, HEAD_DIM)),
        'g_kv_norm': gain(15, (D_MODEL,)),
        'w_kv': nrm(16, (D_MODEL, 2 * KV_W), D_MODEL ** -0.5),
        'g_k': gain(17, (HEAD_DIM,)),
        'g_mem_norm': gain(18, (DEPTH, D_MODEL)),
        'w_mem_kv': nrm(19, (DEPTH, D_MODEL, 2 * MEM_W), D_MODEL ** -0.5),
        'g_mem_q': gain(20, (DEPTH, MEM_HEAD_DIM)),
        'g_mem_k': gain(21, (DEPTH, MEM_HEAD_DIM)),
        'g_ffn_norm': gain(22, (DEPTH, D_MODEL)),
        'w_group': nrm(23, (DEPTH, D_MODEL, N_GROUPS), D_MODEL ** -0.5),
        'b_group': nrm(24, (DEPTH, N_GROUPS), 0.01),
        'w_router': nrm(25, (DEPTH, D_MODEL, N_EXPERTS), D_MODEL ** -0.5),
        'b_router': nrm(26, (DEPTH, N_EXPERTS), 0.01),
        'w_gate_up': nrm(27, (DEPTH, N_EXPERTS, D_MODEL, 2 * D_EXPERT), D_MODEL ** -0.5),
        'w_down': nrm(28, (DEPTH, N_EXPERTS, D_EXPERT, D_MODEL), D_EXPERT ** -0.5),
    }


def reference(x_prompt, x_sample, mem_prompt, cache_k, cache_v, cache_mem_k, cache_mem_v, state_pool,
              page_table, g_mix_norm, w_in, w_out, w_pool, pool_scale, g_q, g_kv_norm, w_kv, g_k,
              g_mem_norm, w_mem_kv, g_mem_q, g_mem_k, g_ffn_norm, w_group, b_group, w_router, b_router,
              w_gate_up, w_down):
    w = dict(g_mix_norm=g_mix_norm, w_in=w_in, w_out=w_out, w_pool=w_pool, pool_scale=pool_scale,
             g_q=g_q, g_kv_norm=g_kv_norm, w_kv=w_kv, g_k=g_k, g_mem_q=g_mem_q, g_ffn_norm=g_ffn_norm,
             w_group=w_group, b_group=b_group, w_router=w_router, b_router=b_router,
             w_gate_up=w_gate_up, w_down=w_down)
    mem_p = [mem_kv(mem_prompt, g_mem_norm[l], w_mem_kv[l], g_mem_k[l]) for l in range(DEPTH)]
    mem_k_prompt = jnp.stack([m[0] for m in mem_p])
    mem_v_prompt = jnp.stack([m[1] for m in mem_p])
    pool_zero = jnp.zeros((N_A_LAYERS, x_prompt.shape[0], POOL_BUF, TOK_W), x_prompt.dtype)
    y_prompt, pool_prompt, k_prompt, v_prompt = _trunk(
        x_prompt, mem_k_prompt, mem_v_prompt, pool_zero, 0, None, None, w)
    nd = x_sample.shape[0]
    past_k = cache_k[page_table].reshape(nd, -1, N_KV_HEADS, HEAD_DIM)
    past_v = cache_v[page_table].reshape(nd, -1, N_KV_HEADS, HEAD_DIM)
    y_sample, pool_sample, k_sample, v_sample = _trunk(
        x_sample, cache_mem_k, cache_mem_v, state_pool, PAST_LEN, past_k, past_v, w)
    return (y_prompt, y_sample, k_prompt, v_prompt, k_sample, v_sample,
            mem_k_prompt, mem_v_prompt, pool_prompt, pool_sample)
```

```python
import functools

import jax
import jax.numpy as jnp
from jax import lax
from jax.experimental import pallas as pl
from jax.experimental.pallas import tpu as pltpu

F32 = jnp.float32
BF16 = jnp.bfloat16

EPS = 1e-6
POOL_WINDOWS = (2, 4, 8, 16)
POOL_BUF = max(POOL_WINDOWS) - 1
POOL_HALO = POOL_BUF + 1
MEM_HEADS = 4
MOBA_BLOCK = 256
MOBA_TOPK = 3
EXPERT_TOPK = 2
ROUTE_LANES = 128
NEG = -1e30
VMEM_LIMIT = 56 * 1024 * 1024

_NT = (((1,), (1,)), ((), ()))


def _params(*sem):
    return pltpu.CompilerParams(dimension_semantics=sem, vmem_limit_bytes=VMEM_LIMIT)


def _split_bf16(a):
    hi = a.astype(BF16)
    lo = (a - hi.astype(F32)).astype(BF16)
    return hi, lo


def _dot3(a, b, dims):
    ah, al = _split_bf16(a)
    bh, bl = _split_bf16(b)
    d = lambda x, y: lax.dot_general(x, y, dims, preferred_element_type=F32)
    return d(ah, bh) + (d(ah, bl) + d(al, bh))


def _rms(x, g):
    return x * lax.rsqrt(jnp.mean(x * x, axis=-1, keepdims=True) + EPS) * g


def _with_alias(body, n_in, alias):
    if alias is None:
        return body, [], [], {}
    wrapped = lambda *refs: body(*refs[:n_in], *refs[n_in + 1:])
    return wrapped, [alias], [pl.BlockSpec(memory_space=pl.ANY)], {"input_output_aliases": {n_in: 0}}


def _norm_matmul_body(x_ref, g_ref, w_ref, gh_ref, o_ref, hn_ref, *, head_groups):
    j = pl.program_id(1)

    @pl.when(j == 0)
    def _():
        hn_ref[...] = _rms(x_ref[...], g_ref[...]).astype(BF16)

    acc = jnp.dot(hn_ref[...], w_ref[...], preferred_element_type=F32)
    tn = acc.shape[1]

    def write(gs):
        if gs == 0:
            o_ref[...] = acc
            return
        for c in range(tn // gs):
            sl = slice(c * gs, (c + 1) * gs)
            o_ref[:, sl] = _rms(acc[:, sl], gh_ref[:, sl])

    if len(set(head_groups)) == 1:
        write(head_groups[0])
    else:
        for jj, gs in enumerate(head_groups):
            pl.when(j == jj)(functools.partial(write, gs))


def _norm_matmul(x, g, w, *, layer, n_out, tm, tn, head_groups=None, gh=None,
                 out_rows=None, row_blk_off=0, alias=None):
    m, k = x.shape
    nj = n_out // tn
    head_groups = tuple(head_groups) if head_groups else (0,) * nj
    if gh is None:
        gh = jnp.ones((1, n_out), F32)
    out_rows = m if out_rows is None else out_rows
    body = functools.partial(_norm_matmul_body, head_groups=head_groups)
    body, extra, extra_specs, kw = _with_alias(body, 4, alias)
    return pl.pallas_call(
        body,
        out_shape=jax.ShapeDtypeStruct((out_rows, n_out), F32),
        grid=(pl.cdiv(m, tm), nj),
        in_specs=[
            pl.BlockSpec((tm, k), lambda i, j: (i, 0)),
            pl.BlockSpec((1, k), lambda i, j: (0, 0)),
            pl.BlockSpec((None, k, tn), lambda i, j: (layer, 0, j)),
            pl.BlockSpec((1, tn), lambda i, j: (0, j)),
        ] + extra_specs,
        out_specs=pl.BlockSpec((tm, tn), lambda i, j: (i + row_blk_off, j)),
        scratch_shapes=[pltpu.VMEM((tm, k), BF16)],
        compiler_params=_params("parallel", "arbitrary"),
        name="norm_matmul",
        **kw,
    )(x, g, w, gh, *extra)


def _pool_body(cur_ref, halo_ref, buf_ref, w_ref, scale_ref, o_ref, *, tt, pos0):
    t = pl.program_id(1)
    cur = cur_ref[...]
    halo = jnp.where(t == 0, buf_ref[...], halo_ref[...])
    full = jnp.concatenate([halo, cur], axis=0)
    pos = pos0 + t * tt + lax.broadcasted_iota(jnp.int32, (tt, 1), 0)
    gw = cur.shape[1] // len(POOL_WINDOWS)
    for gi, win in enumerate(POOL_WINDOWS):
        sl = slice(gi * gw, (gi + 1) * gw)
        s = full[:, sl]
        sh = 1
        while sh < win:
            s = s + pltpu.roll(s, shift=sh, axis=0)
            sh *= 2
        cnt = jnp.minimum(pos + 1, win).astype(F32)
        d = (s[POOL_HALO:] / cnt - cur[:, sl]).astype(BF16)
        y = jnp.dot(d, w_ref[gi], preferred_element_type=F32)
        o_ref[:, sl] = y * scale_ref[:, sl]


def _pool_mix(z, buf16, w_pool, scale, *, n_batch, t_len, tt, row_off, pos0, out_rows, alias=None):
    tok_w = scale.shape[1]
    nt = t_len // tt
    cur_blk = lambda b, t: ((row_off + b * t_len) // tt + t, 0)
    halo_blk = lambda b, t: (jnp.maximum((row_off + b * t_len + t * tt) // POOL_HALO - 1, 0), 0)
    body = functools.partial(_pool_body, tt=tt, pos0=pos0)
    body, extra, extra_specs, kw = _with_alias(body, 5, alias)
    return pl.pallas_call(
        body,
        out_shape=jax.ShapeDtypeStruct((out_rows, tok_w), F32),
        grid=(n_batch, nt),
        in_specs=[
            pl.BlockSpec((tt, tok_w), cur_blk),
            pl.BlockSpec((POOL_HALO, tok_w), halo_blk),
            pl.BlockSpec((None, POOL_HALO, tok_w), lambda b, t: (b, 0, 0)),
            pl.BlockSpec(w_pool.shape, lambda b, t: (0, 0, 0)),
            pl.BlockSpec((1, tok_w), lambda b, t: (0, 0)),
        ] + extra_specs,
        out_specs=pl.BlockSpec((tt, tok_w), cur_blk),
        compiler_params=_params("parallel", "arbitrary"),
        name="pool_mix",
        **kw,
    )(z, z, buf16, w_pool, scale, *extra)


def _mem_attn_body(q_ref, g_ref, k_ref, v_ref, o_ref):
    q = q_ref[...]
    hd = q.shape[1] // MEM_HEADS
    for h in range(MEM_HEADS):
        sl = slice(h * hd, (h + 1) * hd)
        qh = _rms(q[:, sl], g_ref[...]).astype(BF16)
        s = lax.dot_general(qh, k_ref[:, sl].astype(BF16), _NT, preferred_element_type=F32) * (hd ** -0.5)
        e = jnp.exp(s - jnp.max(s, axis=-1, keepdims=True))
        p = (e / jnp.sum(e, axis=-1, keepdims=True)).astype(BF16)
        o_ref[:, sl] = jnp.dot(p, v_ref[:, sl].astype(BF16), preferred_element_type=F32)


def _mem_attn(z, g, k_arr, k_spec, v_arr, v_spec, *, n_batch, t_len, tt, row_off, out_rows, alias=None):
    mem_w = g.shape[1] * MEM_HEADS
    col_blk = z.shape[1] // mem_w - 1
    nt = t_len // tt
    body, extra, extra_specs, kw = _with_alias(_mem_attn_body, 4, alias)
    return pl.pallas_call(
        body,
        out_shape=jax.ShapeDtypeStruct((out_rows, mem_w), F32),
        grid=(n_batch, nt),
        in_specs=[
            pl.BlockSpec((tt, mem_w), lambda b, t: ((row_off + b * t_len) // tt + t, col_blk)),
            pl.BlockSpec((1, g.shape[1]), lambda b, t: (0, 0)),
            k_spec,
            v_spec,
        ] + extra_specs,
        out_specs=pl.BlockSpec((tt, mem_w), lambda b, t: ((row_off + b * t_len) // tt + t, 0)),
        compiler_params=_params("parallel", "arbitrary"),
        name="mem_attn",
        **kw,
    )(z, g, k_arr, v_arr, *extra)


def _out_proj_body(tok_ref, mem_ref, w_ref, x_ref, o_ref, cat_ref):
    tok_w = tok_ref.shape[1]

    @pl.when(pl.program_id(1) == 0)
    def _():
        cat_ref[:, :tok_w] = tok_ref[...].astype(BF16)
        cat_ref[:, tok_w:] = mem_ref[...].astype(BF16)

    o_ref[...] = x_ref[...] + jnp.dot(cat_ref[...], w_ref[...], preferred_element_type=F32)


def _out_proj(tok, mem, w, x, *, layer, tm, tn, in_blk_off=0, out_rows=None, alias=None):
    m, d = x.shape
    tok_w, mem_w = tok.shape[1], mem.shape[1]
    out_rows = m if out_rows is None else out_rows
    body, extra, extra_specs, kw = _with_alias(_out_proj_body, 4, alias)
    return pl.pallas_call(
        body,
        out_shape=jax.ShapeDtypeStruct((out_rows, d), F32),
        grid=(pl.cdiv(m, tm), d // tn),
        in_specs=[
            pl.BlockSpec((tm, tok_w), lambda i, j: (i + in_blk_off, 0)),
            pl.BlockSpec((tm, mem_w), lambda i, j: (i + in_blk_off, 0)),
            pl.BlockSpec((None, tok_w + mem_w, tn), lambda i, j: (layer, 0, j)),
            pl.BlockSpec((tm, tn), lambda i, j: (i, j)),
        ] + extra_specs,
        out_specs=pl.BlockSpec((tm, tn), lambda i, j: (i + in_blk_off, j)),
        scratch_shapes=[pltpu.VMEM((tm, tok_w + mem_w), BF16)],
        compiler_params=_params("parallel", "arbitrary"),
        name="out_proj",
        **kw,
    )(tok, mem, w, x, *extra)


def _route_body(x_ref, g_ref, wh_ref, wl_ref, b_ref, h_ref, info_ref, *, n_groups, per_group):
    h = _rms(x_ref[...], g_ref[...])
    hh, hl = _split_bf16(h)
    h_ref[...] = hh
    d = lambda a, b: jnp.dot(a, b, preferred_element_type=F32)
    logits = d(hh, wh_ref[...]) + (d(hh, wl_ref[...]) + d(hl, wh_ref[...])) + b_ref[...]
    lane = lax.broadcasted_iota(jnp.int32, logits.shape, 1)
    big = jnp.int32(ROUTE_LANES)

    def first_max(mask):
        v = jnp.max(jnp.where(mask, logits, NEG), axis=-1, keepdims=True)
        i = jnp.min(jnp.where(mask & (logits == v), lane, big), axis=-1, keepdims=True)
        return v, i

    g_mask = lane < n_groups
    g_max, g_sel = first_max(g_mask)
    p_group = 1.0 / jnp.sum(jnp.where(g_mask, jnp.exp(logits - g_max), 0.0), axis=-1, keepdims=True)
    lo = n_groups + g_sel * per_group
    e_mask = (lane >= lo) & (lane < lo + per_group)
    v1, i1 = first_max(e_mask)
    v2, i2 = first_max(e_mask & (lane != i1))
    e2 = jnp.exp(v2 - v1)
    w1 = p_group / (1.0 + e2)
    w2 = p_group * e2 / (1.0 + e2)
    id1 = (i1 - n_groups).astype(F32)
    id2 = (i2 - n_groups).astype(F32)
    info_ref[...] = jnp.where(lane == 0, id1, jnp.where(lane == 1, id2,
                              jnp.where(lane == 2, w1, jnp.where(lane == 3, w2, 0.0))))


def _route(x, g, w_hi, w_lo, bias, *, layer, tm, n_groups, per_group):
    m, d = x.shape
    body = functools.partial(_route_body, n_groups=n_groups, per_group=per_group)
    return pl.pallas_call(
        body,
        out_shape=(jax.ShapeDtypeStruct((m, d), BF16), jax.ShapeDtypeStruct((m, ROUTE_LANES), F32)),
        grid=(pl.cdiv(m, tm),),
        in_specs=[
            pl.BlockSpec((tm, d), lambda i: (i, 0)),
            pl.BlockSpec((1, d), lambda i: (0, 0)),
            pl.BlockSpec((None, d, ROUTE_LANES), lambda i: (layer, 0, 0)),
            pl.BlockSpec((None, d, ROUTE_LANES), lambda i: (layer, 0, 0)),
            pl.BlockSpec((None, 1, ROUTE_LANES), lambda i: (layer, 0, 0)),
        ],
        out_specs=(pl.BlockSpec((tm, d), lambda i: (i, 0)), pl.BlockSpec((tm, ROUTE_LANES), lambda i: (i, 0))),
        compiler_params=_params("parallel"),
        name="moe_route",
    )(x, g, w_hi, w_lo, bias)


def _expert_body(te_ref, tb_ref, tv_ref, xs_ref, wg_ref, wu_ref, wd_ref, sw_ref, o_ref):
    t, c = pl.program_id(0), pl.program_id(1)
    last = pl.num_programs(1) - 1

    @pl.when(tv_ref[t] == 1)
    def _():
        x = xs_ref[...]
        gate = jnp.dot(x, wg_ref[...], preferred_element_type=F32)
        up = jnp.dot(x, wu_ref[...], preferred_element_type=F32)
        act = (gate * jax.nn.sigmoid(gate) * up).astype(BF16)
        y = jnp.dot(act, wd_ref[...], preferred_element_type=F32)

        @pl.when(c == 0)
        def _():
            o_ref[...] = y

        @pl.when(c > 0)
        def _():
            o_ref[...] += y

        @pl.when(c == last)
        def _():
            o_ref[...] *= sw_ref[...]


def _experts(xs, slot_w, w_gate_up, w_down, tile_expert, tile_block, tile_valid, *, layer, tm, fc):
    s_rows, d = xs.shape
    f = w_down.shape[2]
    nc = f // fc
    n_tiles = tile_expert.shape[0]
    cc = lambda t, c, tv: jnp.where(tv[t] == 1, c, nc - 1)
    grid_spec = pltpu.PrefetchScalarGridSpec(
        num_scalar_prefetch=3,
        grid=(n_tiles, nc),
        in_specs=[
            pl.BlockSpec((tm, d), lambda t, c, te, tb, tv: (tb[t], 0)),
            pl.BlockSpec((None, None, d, fc), lambda t, c, te, tb, tv: (layer, te[t], 0, cc(t, c, tv))),
            pl.BlockSpec((None, None, d, fc), lambda t, c, te, tb, tv: (layer, te[t], 0, nc + cc(t, c, tv))),
            pl.BlockSpec((None, None, fc, d), lambda t, c, te, tb, tv: (layer, te[t], cc(t, c, tv), 0)),
            pl.BlockSpec((tm, 1), lambda t, c, te, tb, tv: (tb[t], 0)),
        ],
        out_specs=pl.BlockSpec((tm, d), lambda t, c, te, tb, tv: (tb[t], 0)),
    )
    return pl.pallas_call(
        _expert_body,
        out_shape=jax.ShapeDtypeStruct((s_rows, d), F32),
        grid_spec=grid_spec,
        compiler_params=_params("arbitrary", "arbitrary"),
        name="moe_experts",
    )(tile_expert, tile_block, tile_valid, xs, w_gate_up, w_gate_up, w_down, slot_w)


def _moe_plan(info, n_experts, tm):
    m = info.shape[0]
    ids = info[:, :EXPERT_TOPK].astype(jnp.int32).reshape(-1)
    wts = info[:, EXPERT_TOPK:2 * EXPERT_TOPK].reshape(-1)
    n_pairs = m * EXPERT_TOPK
    n_tiles = n_pairs // tm + n_experts
    onehot = (ids[:, None] == jnp.arange(n_experts)[None, :]).astype(jnp.int32)
    csum = jnp.cumsum(onehot, axis=0)
    rank = jnp.take_along_axis(csum, ids[:, None], axis=1)[:, 0] - 1
    counts = csum[-1]
    tiles_e = (counts + tm - 1) // tm
    tile_end = jnp.cumsum(tiles_e)
    tile_start = tile_end - tiles_e
    dest = tile_start[ids] * tm + rank
    total = tile_end[-1]
    t_idx = jnp.arange(n_tiles, dtype=jnp.int32)
    t_eff = jnp.minimum(t_idx, total - 1)
    tile_expert = jnp.searchsorted(tile_end, t_eff, side="right").astype(jnp.int32)
    tile_valid = (t_idx < total).astype(jnp.int32)
    slot_token = jnp.zeros((n_tiles * tm,), jnp.int32).at[dest].set(jnp.arange(n_pairs, dtype=jnp.int32) // EXPERT_TOPK)
    slot_w = jnp.zeros((n_tiles * tm,), F32).at[dest].set(wts)
    return slot_token, slot_w[:, None], dest, tile_expert, t_eff.astype(jnp.int32), tile_valid


def _moe(x, g, w_rt_hi, w_rt_lo, b_rt, w_gate_up, w_down, *, layer, n_groups, per_group, tm_route, tm, fc):
    n_experts = n_groups * per_group
    h, info = _route(x, g, w_rt_hi, w_rt_lo, b_rt, layer=layer, tm=tm_route, n_groups=n_groups, per_group=per_group)
    slot_token, slot_w, dest, te, tb, tv = _moe_plan(info, n_experts, tm)
    xs = h[slot_token]
    ys = _experts(xs, slot_w, w_gate_up, w_down, te, tb, tv, layer=layer, tm=tm, fc=fc)
    contrib = ys[dest].reshape(x.shape[0], EXPERT_TOPK, x.shape[1])
    return x + contrib[:, 0] + contrib[:, 1]


def _moba_prompt_body(slopes_ref, q_ref, gq_ref, k_ref, v_ref, o_ref, *, kv_group, blk):
    g, own = pl.program_id(1), pl.program_id(2)
    hd = k_ref.shape[1]
    nb = k_ref.shape[0] // blk
    rows = kv_group * blk
    q = q_ref[...]
    qn = jnp.concatenate([_rms(q[:, r * hd:(r + 1) * hd], gq_ref[...]) for r in range(kv_group)], axis=0)
    qb = qn.astype(BF16)
    row = lax.broadcasted_iota(jnp.int32, (rows, 1), 0)
    slope = jnp.zeros((rows, 1), F32)
    for r in range(kv_group):
        slope = jnp.where(row // blk == r, slopes_ref[g * kv_group + r], slope)
    trow = row % blk
    col = lax.broadcasted_iota(jnp.int32, (1, blk), 1)
    rel = slope * (trow - col).astype(F32)
    scale = hd ** -0.5

    km = jnp.concatenate([jnp.mean(k_ref[n * blk:(n + 1) * blk, :], axis=0, keepdims=True) for n in range(nb)]
                         + [jnp.zeros((ROUTE_LANES - nb, hd), F32)], axis=0)
    lane = lax.broadcasted_iota(jnp.int32, (rows, ROUTE_LANES), 1)
    gate = jnp.where(lane < own, _dot3(qn, km, _NT), NEG)
    cnt = jnp.zeros(gate.shape, jnp.int32)
    for m_ in range(nb):
        gm = gate[:, m_:m_ + 1]
        cnt = cnt + ((gm > gate) | ((gm == gate) & (m_ < lane))).astype(jnp.int32)
    sel = (cnt < MOBA_TOPK) & (lane < own)

    def block_scores(n):
        start = pl.multiple_of(n * blk, blk)
        kb = k_ref[pl.ds(start, blk), :].astype(BF16)
        vb = v_ref[pl.ds(start, blk), :].astype(BF16)
        s = lax.dot_general(qb, kb, _NT, preferred_element_type=F32) * scale
        return s - rel, vb

    s, vb = block_scores(own)
    s = jnp.where(col <= trow, s, NEG)
    m0 = jnp.max(s, axis=-1, keepdims=True)
    p = jnp.exp(s - m0)
    l0 = jnp.sum(p, axis=-1, keepdims=True)
    a0 = jnp.dot(p.astype(BF16), vb, preferred_element_type=F32)

    def step(n, carry):
        m_run, l_run, acc = carry
        s, vb = block_scores(n)
        picked = jnp.sum(jnp.where((lane == n) & sel, 1.0, 0.0), axis=-1, keepdims=True) > 0.0
        far = slope * ((own - n) * blk).astype(F32)
        s = jnp.where(picked, s - far, NEG)
        m_new = jnp.maximum(m_run, jnp.max(s, axis=-1, keepdims=True))
        alpha = jnp.exp(m_run - m_new)
        p = jnp.exp(s - m_new)
        l_new = alpha * l_run + jnp.sum(p, axis=-1, keepdims=True)
        acc = alpha * acc + jnp.dot(p.astype(BF16), vb, preferred_element_type=F32)
        return m_new, l_new, acc

    _, l_fin, acc = lax.fori_loop(0, own, step, (m0, l0, a0))
    out = acc / l_fin
    for r in range(kv_group):
        o_ref[:, r * hd:(r + 1) * hd] = out[r * blk:(r + 1) * blk]


def _moba_prompt(z, gq, kv, slopes, *, n_batch, t_len, n_kv, kv_group, out_rows):
    hd = gq.shape[1]
    blk = MOBA_BLOCK
    gw = kv_group * hd
    nt = t_len // blk
    body = functools.partial(_moba_prompt_body, kv_group=kv_group, blk=blk)
    grid_spec = pltpu.PrefetchScalarGridSpec(
        num_scalar_prefetch=1,
        grid=(n_batch, n_kv, nt),
        in_specs=[
            pl.BlockSpec((blk, gw), lambda b, g, t, s: (b * nt + t, g)),
            pl.BlockSpec((1, hd), lambda b, g, t, s: (0, 0)),
            pl.BlockSpec((t_len, hd), lambda b, g, t, s: (b, g)),
            pl.BlockSpec((t_len, hd), lambda b, g, t, s: (b, n_kv + g)),
        ],
        out_specs=pl.BlockSpec((blk, gw), lambda b, g, t, s: (b * nt + t, g)),
    )
    return pl.pallas_call(
        body,
        out_shape=jax.ShapeDtypeStruct((out_rows, n_kv * gw), F32),
        grid_spec=grid_spec,
        compiler_params=_params("parallel", "parallel", "arbitrary"),
        name="moba_prompt",
    )(slopes, z, gq, kv, kv)


def _moba_partial_body(pt_ref, slopes_ref, q_ref, gq_ref, k0_ref, k1_ref, v0_ref, v1_ref, po_ref, ps_ref,
                       *, n_kv, kv_group, past_len):
    n = pl.program_id(1)
    hd = gq_ref.shape[1]
    t_len = q_ref.shape[0]
    rows = kv_group * t_len
    blk = 2 * k0_ref.shape[0]
    q = q_ref[...]
    row = lax.broadcasted_iota(jnp.int32, (rows, 1), 0)
    col = lax.broadcasted_iota(jnp.int32, (1, blk), 1)
    dist = (past_len + row % t_len - n * blk - col).astype(F32)
    lane = lax.broadcasted_iota(jnp.int32, (rows, ROUTE_LANES), 1)
    stats = jnp.zeros((rows, ROUTE_LANES), F32)
    scale = hd ** -0.5
    for g in range(n_kv):
        qn = jnp.concatenate([_rms(q[:, (g * kv_group + r) * hd:(g * kv_group + r + 1) * hd], gq_ref[...])
                              for r in range(kv_group)], axis=0)
        slope = jnp.zeros((rows, 1), F32)
        for r in range(kv_group):
            slope = jnp.where(row // t_len == r, slopes_ref[g * kv_group + r], slope)
        sl = slice(g * hd, (g + 1) * hd)
        kb = jnp.concatenate([k0_ref[:, sl], k1_ref[:, sl]], axis=0)
        vb = jnp.concatenate([v0_ref[:, sl], v1_ref[:, sl]], axis=0).astype(BF16)
        km = jnp.mean(kb, axis=0, keepdims=True)
        gate = jnp.sum(qn * km, axis=-1, keepdims=True)
        s = lax.dot_general(qn.astype(BF16), kb.astype(BF16), _NT, preferred_element_type=F32) * scale - slope * dist
        mx = jnp.max(s, axis=-1, keepdims=True)
        p = jnp.exp(s - mx)
        den = jnp.sum(p, axis=-1, keepdims=True)
        po_ref[:, sl] = jnp.dot(p.astype(BF16), vb, preferred_element_type=F32)
        stats = jnp.where(lane == g, mx, jnp.where(lane == n_kv + g, den, jnp.where(lane == 2 * n_kv + g, gate, stats)))
    ps_ref[...] = stats


def _moba_partials(z, gq, cache_k, cache_v, page_table, slopes, *, n_batch, t_len, row_off, n_kv, kv_group, past_len):
    hd = gq.shape[1]
    page = cache_k.shape[1]
    pages_per_blk = MOBA_BLOCK // page
    assert pages_per_blk == 2
    n_pages = page_table.shape[1]
    nblk = n_pages // pages_per_blk
    kw = n_kv * hd
    rows = kv_group * t_len
    body = functools.partial(_moba_partial_body, n_kv=n_kv, kv_group=kv_group, past_len=past_len)
    page_spec = lambda which: pl.BlockSpec(
        (None, page, kw), lambda b, n, pt, s: (pt[b * n_pages + n * pages_per_blk + which], 0, 0))
    grid_spec = pltpu.PrefetchScalarGridSpec(
        num_scalar_prefetch=2,
        grid=(n_batch, nblk),
        in_specs=[
            pl.BlockSpec((t_len, kv_group * kw), lambda b, n, pt, s: (row_off // t_len + b, 0)),
            pl.BlockSpec((1, hd), lambda b, n, pt, s: (0, 0)),
            page_spec(0), page_spec(1), page_spec(0), page_spec(1),
        ],
        out_specs=(
            pl.BlockSpec((None, None, rows, kw), lambda b, n, pt, s: (b, n, 0, 0)),
            pl.BlockSpec((None, None, rows, ROUTE_LANES), lambda b, n, pt, s: (b, n, 0, 0)),
        ),
    )
    return pl.pallas_call(
        body,
        out_shape=(jax.ShapeDtypeStruct((n_batch, nblk, rows, kw), F32),
                   jax.ShapeDtypeStruct((n_batch, nblk, rows, ROUTE_LANES), F32)),
        grid_spec=grid_spec,
        compiler_params=_params("parallel", "arbitrary"),
        name="moba_partials",
    )(page_table.reshape(-1), slopes, z, gq, cache_k, cache_k, cache_v, cache_v)


def _moba_merge_body(slopes_ref, q_ref, gq_ref, kn_ref, vn_ref, po_ref, ps_ref, o_ref, *, n_kv, kv_group):
    hd = gq_ref.shape[1]
    t_len = q_ref.shape[0]
    rows = kv_group * t_len
    nblk = ps_ref.shape[0]
    q = q_ref[...]
    stats = ps_ref[...]
    blk_id = lax.broadcasted_iota(jnp.int32, stats.shape, 0)
    gates = stats
    sel = jnp.zeros(stats.shape, jnp.bool_)
    for _ in range(min(MOBA_TOPK, nblk)):
        best = jnp.max(gates, axis=0, keepdims=True)
        first = jnp.min(jnp.where(gates == best, blk_id, nblk), axis=0, keepdims=True)
        hit = blk_id == first
        sel = sel | hit
        gates = jnp.where(hit, NEG, gates)
    row = lax.broadcasted_iota(jnp.int32, (rows, 1), 0)
    tq = row % t_len
    tk = lax.broadcasted_iota(jnp.int32, (1, t_len), 1)
    scale = hd ** -0.5
    for g in range(n_kv):
        qn = jnp.concatenate([_rms(q[:, (g * kv_group + r) * hd:(g * kv_group + r + 1) * hd], gq_ref[...])
                              for r in range(kv_group)], axis=0)
        slope = jnp.zeros((rows, 1), F32)
        for r in range(kv_group):
            slope = jnp.where(row // t_len == r, slopes_ref[g * kv_group + r], slope)
        sl = slice(g * hd, (g + 1) * hd)
        s = lax.dot_general(qn.astype(BF16), kn_ref[:, sl].astype(BF16), _NT, preferred_element_type=F32) * scale
        s = jnp.where(tk <= tq, s - slope * (tq - tk).astype(F32), NEG)
        m_own = jnp.max(s, axis=-1, keepdims=True)
        picked = sel[:, :, 2 * n_kv + g:2 * n_kv + g + 1]
        m_blk = jnp.where(picked, stats[:, :, g:g + 1], NEG)
        m_all = jnp.maximum(m_own, jnp.max(m_blk, axis=0))
        w_blk = jnp.where(picked, jnp.exp(m_blk - m_all[None]), 0.0)
        p = jnp.exp(s - m_all)
        den = jnp.sum(p, axis=-1, keepdims=True) + jnp.sum(w_blk * stats[:, :, n_kv + g:n_kv + g + 1], axis=0)
        num = jnp.dot(p.astype(BF16), vn_ref[:, sl].astype(BF16), preferred_element_type=F32)
        num = num + jnp.sum(w_blk * po_ref[:, :, sl], axis=0)
        out = num / den
        for r in range(kv_group):
            h = g * kv_group + r
            o_ref[:, h * hd:(h + 1) * hd] = out[r * t_len:(r + 1) * t_len]


def _moba_merge(z, gq, kv, part_o, part_s, slopes, *, n_batch, t_len, row_off, n_kv, kv_group, alias):
    hd = gq.shape[1]
    kw = n_kv * hd
    rows = kv_group * t_len
    nblk = part_o.shape[1]
    body = functools.partial(_moba_merge_body, n_kv=n_kv, kv_group=kv_group)
    wrapped = lambda *refs: body(*refs[:7], *refs[8:])
    new_rows = lambda b, s: (row_off // t_len + b, 0)
    grid_spec = pltpu.PrefetchScalarGridSpec(
        num_scalar_prefetch=1,
        grid=(n_batch,),
        in_specs=[
            pl.BlockSpec((t_len, kv_group * kw), new_rows),
            pl.BlockSpec((1, hd), lambda b, s: (0, 0)),
            pl.BlockSpec((t_len, kw), new_rows),
            pl.BlockSpec((t_len, kw), lambda b, s: (row_off // t_len + b, 1)),
            pl.BlockSpec((None, nblk, rows, kw), lambda b, s: (b, 0, 0, 0)),
            pl.BlockSpec((None, nblk, rows, ROUTE_LANES), lambda b, s: (b, 0, 0, 0)),
            pl.BlockSpec(memory_space=pl.ANY),
        ],
        out_specs=pl.BlockSpec((t_len, kv_group * kw), new_rows),
    )
    return pl.pallas_call(
        wrapped,
        out_shape=jax.ShapeDtypeStruct(alias.shape, F32),
        grid_spec=grid_spec,
        input_output_aliases={7: 0},
        compiler_params=_params("parallel"),
        name="moba_merge",
    )(slopes, z, gq, kv, kv, part_o, part_s, alias)


def _tile(m, pref):
    return pref if m >= pref else m


def kernel(x_prompt, x_sample, mem_prompt, cache_k, cache_v, cache_mem_k, cache_mem_v, state_pool, page_table, g_mix_norm, w_in, w_out, w_pool, pool_scale, g_q, g_kv_norm, w_kv, g_k, g_mem_norm, w_mem_kv, g_mem_q, g_mem_k, g_ffn_norm, w_group, b_group, w_router, b_router, w_gate_up, w_down):
    bp, tp, d = x_prompt.shape
    bs, ts, _ = x_sample.shape
    depth = w_in.shape[0]
    n_a = w_pool.shape[0]
    mem_tokens = mem_prompt.shape[1]
    mem_w = d // 4
    mem_hd = mem_w // MEM_HEADS
    tok_w = d - mem_w
    page, n_kv, hd = cache_k.shape[1:]
    kv_w = n_kv * hd
    n_heads = tok_w // hd
    kv_group = n_heads // n_kv
    past_len = page_table.shape[1] * page
    n_groups = w_group.shape[2]
    n_experts = w_router.shape[2]
    per_group = n_experts // n_groups
    d_expert = w_down.shape[2]
    mp, ms = bp * tp, bs * ts
    m_all = mp + ms

    tm_p = _tile(mp, 512)
    tn = _tile(d, 1024)
    tt_p = _tile(tp, MOBA_BLOCK)
    fc = _tile(d_expert, 256)
    tm_e = 512 if m_all >= 4096 else 128

    row = lambda v: v.reshape(1, -1).astype(F32)
    w_in_b, w_out_b, w_pool_b = w_in.astype(BF16), w_out.astype(BF16), w_pool.astype(BF16)
    w_kv_b, w_mem_kv_b = w_kv.astype(BF16)[None], w_mem_kv.astype(BF16)
    w_gu_b, w_dn_b = w_gate_up.astype(BF16), w_down.astype(BF16)
    w_rt = jnp.concatenate([w_group, w_router, jnp.zeros((depth, d, ROUTE_LANES - n_groups - n_experts), F32)], axis=-1)
    w_rt_hi = w_rt.astype(BF16)
    w_rt_lo = (w_rt - w_rt_hi.astype(F32)).astype(BF16)
    b_rt = jnp.concatenate([b_group, b_router, jnp.zeros((depth, ROUTE_LANES - n_groups - n_experts), F32)],
                           axis=-1)[:, None, :]
    slopes = jnp.exp2(-8.0 * jnp.arange(1, n_heads + 1, dtype=F32) / n_heads)

    xp = x_prompt.reshape(mp, d)
    xs_ = x_sample.reshape(ms, d)
    sample_blk = mp // ms

    memx = mem_prompt.reshape(bp * mem_tokens, d)
    mem_kv_p = [
        _norm_matmul(memx, row(g_mem_norm[l]), w_mem_kv_b, layer=l, n_out=2 * mem_w, tm=_tile(bp * mem_tokens, 512),
                     tn=mem_w, head_groups=(mem_hd, 0),
                     gh=jnp.concatenate([jnp.tile(row(g_mem_k[l]), (1, MEM_HEADS)), jnp.ones((1, mem_w), F32)], axis=1))
        for l in range(depth)]
    mem_kv_p3 = [a.reshape(bp, mem_tokens, 2 * mem_w) for a in mem_kv_p]
    cmk = cache_mem_k.reshape(depth, bs, mem_tokens, mem_w)
    cmv = cache_mem_v.reshape(depth, bs, mem_tokens, mem_w)

    def mem_attend(z, l):
        y = _mem_attn(z, row(g_mem_q[l]),
                      mem_kv_p3[l], pl.BlockSpec((None, mem_tokens, mem_w), lambda b, t: (b, 0, 0)),
                      mem_kv_p3[l], pl.BlockSpec((None, mem_tokens, mem_w), lambda b, t: (b, 0, 1)),
                      n_batch=bp, t_len=tp, tt=tt_p, row_off=0, out_rows=m_all)
        return _mem_attn(z, row(g_mem_q[l]),
                         cmk, pl.BlockSpec((None, None, mem_tokens, mem_w), lambda b, t: (l, b, 0, 0)),
                         cmv, pl.BlockSpec((None, None, mem_tokens, mem_w), lambda b, t: (l, b, 0, 0)),
                         n_batch=bs, t_len=ts, tt=ts, row_off=mp, out_rows=m_all, alias=y)

    def moe(x, l):
        return _moe(x, row(g_ffn_norm[l]), w_rt_hi, w_rt_lo, b_rt, w_gu_b, w_dn_b, layer=l, n_groups=n_groups,
                    per_group=per_group, tm_route=tm_p, tm=tm_e, fc=fc)

    x = None
    z_pool = []
    kv = None
    for l in range(depth):
        g_mix = row(g_mix_norm[l])
        if l == 0:
            z = _norm_matmul(xp, g_mix, w_in_b, layer=l, n_out=d, tm=tm_p, tn=tn, out_rows=m_all)
            z = _norm_matmul(xs_, g_mix, w_in_b, layer=l, n_out=d, tm=ms, tn=tn, out_rows=m_all,
                             row_blk_off=sample_blk, alias=z)
        else:
            z = _norm_matmul(x, g_mix, w_in_b, layer=l, n_out=d, tm=tm_p, tn=tn)
        mem_y = mem_attend(z, l)
        if l < n_a:
            z_pool.append(z)
            scale = row(pool_scale[l])
            zero_buf = jnp.zeros((bp, POOL_HALO, tok_w), F32)
            samp_buf = jnp.concatenate([jnp.zeros((bs, 1, tok_w), F32), state_pool[l]], axis=1)
            tok_y = _pool_mix(z, zero_buf, w_pool_b[l], scale, n_batch=bp, t_len=tp, tt=tt_p, row_off=0, pos0=0,
                              out_rows=m_all)
            tok_y = _pool_mix(z, samp_buf, w_pool_b[l], scale, n_batch=bs, t_len=ts, tt=ts, row_off=mp,
                              pos0=past_len, out_rows=m_all, alias=tok_y)
        else:
            if kv is None:
                kv = _norm_matmul(x, row(g_kv_norm), w_kv_b, layer=0, n_out=2 * kv_w, tm=tm_p, tn=kv_w,
                                  head_groups=(hd, 0),
                                  gh=jnp.concatenate([jnp.tile(row(g_k), (1, n_kv)), jnp.ones((1, kv_w), F32)], axis=1))
            gq = row(g_q[l - n_a])
            tok_y = _moba_prompt(z, gq, kv, slopes, n_batch=bp, t_len=tp, n_kv=n_kv, kv_group=kv_group,
                                 out_rows=m_all)
            ck = cache_k.reshape(cache_k.shape[0], page, kv_w)
            cv = cache_v.reshape(cache_v.shape[0], page, kv_w)
            part_o, part_s = _moba_partials(z, gq, ck, cv, page_table, slopes, n_batch=bs, t_len=ts, row_off=mp,
                                            n_kv=n_kv, kv_group=kv_group, past_len=past_len)
            tok_y = _moba_merge(z, gq, kv, part_o, part_s, slopes, n_batch=bs, t_len=ts, row_off=mp,
                                n_kv=n_kv, kv_group=kv_group, alias=tok_y)
        if l == 0:
            x = _out_proj(tok_y, mem_y, w_out_b, xp, layer=l, tm=tm_p, tn=tn, out_rows=m_all)
            x = _out_proj(tok_y, mem_y, w_out_b, xs_, layer=l, tm=ms, tn=tn, in_blk_off=sample_blk, out_rows=m_all,
                          alias=x)
        else:
            x = _out_proj(tok_y, mem_y, w_out_b, x, layer=l, tm=tm_p, tn=tn)
        x = moe(x, l)

    y_prompt = x[:mp].reshape(bp, tp, d)
    y_sample = x[mp:].reshape(bs, ts, d)
    k_prompt = kv[:mp, :kv_w].reshape(bp, tp, n_kv, hd)
    v_prompt = kv[:mp, kv_w:].reshape(bp, tp, n_kv, hd)
    k_sample = kv[mp:, :kv_w].reshape(bs, ts, n_kv, hd)
    v_sample = kv[mp:, kv_w:].reshape(bs, ts, n_kv, hd)
    mem_k_prompt = jnp.stack([a[:, :, :mem_w].reshape(bp, mem_tokens, MEM_HEADS, mem_hd) for a in mem_kv_p3])
    mem_v_prompt = jnp.stack([a[:, :, mem_w:].reshape(bp, mem_tokens, MEM_HEADS, mem_hd) for a in mem_kv_p3])
    pool_prompt = jnp.stack([zz[:mp].reshape(bp, tp, d)[:, tp - POOL_BUF:, :tok_w] for zz in z_pool])
    pool_sample = jnp.stack([
        jnp.concatenate([state_pool[i], zz[mp:].reshape(bs, ts, d)[:, :, :tok_w]], axis=1)[:, -POOL_BUF:]
        for i, zz in enumerate(z_pool)])
    return (y_prompt, y_sample, k_prompt, v_prompt, k_sample, v_sample,
            mem_k_prompt, mem_v_prompt, pool_prompt, pool_sample)
```

```python
import functools

import jax
import jax.numpy as jnp
from jax import lax
from jax.experimental import pallas as pl
from jax.experimental.pallas import tpu as pltpu

F32 = jnp.float32
BF16 = jnp.bfloat16

EPS = 1e-6
POOL_WINDOWS = (2, 4, 8, 16)
POOL_BUF = max(POOL_WINDOWS) - 1
POOL_HALO = POOL_BUF + 1
MEM_HEADS = 4
MOBA_BLOCK = 256
MOBA_TOPK = 3
EXPERT_TOPK = 2
ROUTE_LANES = 128
NEG = -1e30
VMEM_LIMIT = 56 * 1024 * 1024
DMA_UNROLL = 8

_NT = (((1,), (1,)), ((), ()))


def _params(*sem):
    return pltpu.CompilerParams(dimension_semantics=sem, vmem_limit_bytes=VMEM_LIMIT)


def _split_bf16(a):
    hi = a.astype(BF16)
    lo = (a - hi.astype(F32)).astype(BF16)
    return hi, lo


def _dot3(a, b, dims):
    ah, al = _split_bf16(a)
    bh, bl = _split_bf16(b)
    d = lambda x, y: lax.dot_general(x, y, dims, preferred_element_type=F32)
    return d(ah, bh) + (d(ah, bl) + d(al, bh))


def _rms(x, g):
    return x * lax.rsqrt(jnp.mean(x * x, axis=-1, keepdims=True) + EPS) * g


def _with_alias(body, n_in, alias):
    if alias is None:
        return body, [], [], {}
    wrapped = lambda *refs: body(*refs[:n_in], *refs[n_in + 1:])
    return wrapped, [alias], [pl.BlockSpec(memory_space=pl.ANY)], {"input_output_aliases": {n_in: 0}}


def _norm_matmul_body(x_ref, g_ref, w_ref, gh_ref, o_ref, hn_ref, *, head_groups):
    j = pl.program_id(1)

    @pl.when(j == 0)
    def _():
        hn_ref[...] = _rms(x_ref[...], g_ref[...]).astype(BF16)

    acc = jnp.dot(hn_ref[...], w_ref[...], preferred_element_type=F32)
    tn = acc.shape[1]

    def write(gs):
        if gs == 0:
            o_ref[...] = acc
            return
        for c in range(tn // gs):
            sl = slice(c * gs, (c + 1) * gs)
            o_ref[:, sl] = _rms(acc[:, sl], gh_ref[:, sl])

    if len(set(head_groups)) == 1:
        write(head_groups[0])
    else:
        for jj, gs in enumerate(head_groups):
            pl.when(j == jj)(functools.partial(write, gs))


def _norm_matmul(x, g, w, *, layer, n_out, tm, tn, head_groups=None, gh=None,
                 out_rows=None, row_blk_off=0, alias=None):
    m, k = x.shape
    nj = n_out // tn
    head_groups = tuple(head_groups) if head_groups else (0,) * nj
    if gh is None:
        gh = jnp.ones((1, n_out), F32)
    out_rows = m if out_rows is None else out_rows
    body = functools.partial(_norm_matmul_body, head_groups=head_groups)
    body, extra, extra_specs, kw = _with_alias(body, 4, alias)
    return pl.pallas_call(
        body,
        out_shape=jax.ShapeDtypeStruct((out_rows, n_out), F32),
        grid=(pl.cdiv(m, tm), nj),
        in_specs=[
            pl.BlockSpec((tm, k), lambda i, j: (i, 0)),
            pl.BlockSpec((1, k), lambda i, j: (0, 0)),
            pl.BlockSpec((None, k, tn), lambda i, j: (layer, 0, j)),
            pl.BlockSpec((1, tn), lambda i, j: (0, j)),
        ] + extra_specs,
        out_specs=pl.BlockSpec((tm, tn), lambda i, j: (i + row_blk_off, j)),
        scratch_shapes=[pltpu.VMEM((tm, k), BF16)],
        compiler_params=_params("parallel", "arbitrary"),
        name="norm_matmul",
        **kw,
    )(x, g, w, gh, *extra)


def _pool_body(cur_ref, halo_ref, buf_ref, w_ref, scale_ref, o_ref, *, tt, pos0):
    t = pl.program_id(1)
    cur = cur_ref[...]
    halo = jnp.where(t == 0, buf_ref[...], halo_ref[...])
    full = jnp.concatenate([halo, cur], axis=0)
    pos = pos0 + t * tt + lax.broadcasted_iota(jnp.int32, (tt, 1), 0)
    gw = cur.shape[1] // len(POOL_WINDOWS)
    for gi, win in enumerate(POOL_WINDOWS):
        sl = slice(gi * gw, (gi + 1) * gw)
        s = full[:, sl]
        sh = 1
        while sh < win:
            s = s + pltpu.roll(s, shift=sh, axis=0)
            sh *= 2
        cnt = jnp.minimum(pos + 1, win).astype(F32)
        d = (s[POOL_HALO:] / cnt - cur[:, sl]).astype(BF16)
        y = jnp.dot(d, w_ref[gi], preferred_element_type=F32)
        o_ref[:, sl] = y * scale_ref[:, sl]


def _pool_mix(z, buf16, w_pool, scale, *, n_batch, t_len, tt, row_off, pos0, out_rows, alias=None):
    tok_w = scale.shape[1]
    nt = t_len // tt
    cur_blk = lambda b, t: ((row_off + b * t_len) // tt + t, 0)
    halo_blk = lambda b, t: (jnp.maximum((row_off + b * t_len + t * tt) // POOL_HALO - 1, 0), 0)
    body = functools.partial(_pool_body, tt=tt, pos0=pos0)
    body, extra, extra_specs, kw = _with_alias(body, 5, alias)
    return pl.pallas_call(
        body,
        out_shape=jax.ShapeDtypeStruct((out_rows, tok_w), F32),
        grid=(n_batch, nt),
        in_specs=[
            pl.BlockSpec((tt, tok_w), cur_blk),
            pl.BlockSpec((POOL_HALO, tok_w), halo_blk),
            pl.BlockSpec((None, POOL_HALO, tok_w), lambda b, t: (b, 0, 0)),
            pl.BlockSpec(w_pool.shape, lambda b, t: (0, 0, 0)),
            pl.BlockSpec((1, tok_w), lambda b, t: (0, 0)),
        ] + extra_specs,
        out_specs=pl.BlockSpec((tt, tok_w), cur_blk),
        compiler_params=_params("parallel", "arbitrary"),
        name="pool_mix",
        **kw,
    )(z, z, buf16, w_pool, scale, *extra)


def _mem_attn_body(q_ref, g_ref, k_ref, v_ref, o_ref):
    q = q_ref[...]
    hd = q.shape[1] // MEM_HEADS
    for h in range(MEM_HEADS):
        sl = slice(h * hd, (h + 1) * hd)
        qh = _rms(q[:, sl], g_ref[...]).astype(BF16)
        s = lax.dot_general(qh, k_ref[:, sl].astype(BF16), _NT, preferred_element_type=F32) * (hd ** -0.5)
        e = jnp.exp(s - jnp.max(s, axis=-1, keepdims=True))
        p = (e / jnp.sum(e, axis=-1, keepdims=True)).astype(BF16)
        o_ref[:, sl] = jnp.dot(p, v_ref[:, sl].astype(BF16), preferred_element_type=F32)


def _mem_attn(z, g, k_arr, k_spec, v_arr, v_spec, *, n_batch, t_len, tt, row_off, out_rows, alias=None):
    mem_w = g.shape[1] * MEM_HEADS
    col_blk = z.shape[1] // mem_w - 1
    nt = t_len // tt
    body, extra, extra_specs, kw = _with_alias(_mem_attn_body, 4, alias)
    return pl.pallas_call(
        body,
        out_shape=jax.ShapeDtypeStruct((out_rows, mem_w), F32),
        grid=(n_batch, nt),
        in_specs=[
            pl.BlockSpec((tt, mem_w), lambda b, t: ((row_off + b * t_len) // tt + t, col_blk)),
            pl.BlockSpec((1, g.shape[1]), lambda b, t: (0, 0)),
            k_spec,
            v_spec,
        ] + extra_specs,
        out_specs=pl.BlockSpec((tt, mem_w), lambda b, t: ((row_off + b * t_len) // tt + t, 0)),
        compiler_params=_params("parallel", "arbitrary"),
        name="mem_attn",
        **kw,
    )(z, g, k_arr, v_arr, *extra)


def _out_proj_body(tok_ref, mem_ref, w_ref, x_ref, o_ref, cat_ref):
    tok_w = tok_ref.shape[1]

    @pl.when(pl.program_id(1) == 0)
    def _():
        cat_ref[:, :tok_w] = tok_ref[...].astype(BF16)
        cat_ref[:, tok_w:] = mem_ref[...].astype(BF16)

    o_ref[...] = x_ref[...] + jnp.dot(cat_ref[...], w_ref[...], preferred_element_type=F32)


def _out_proj(tok, mem, w, x, *, layer, tm, tn, in_blk_off=0, out_rows=None, alias=None):
    m, d = x.shape
    tok_w, mem_w = tok.shape[1], mem.shape[1]
    out_rows = m if out_rows is None else out_rows
    body, extra, extra_specs, kw = _with_alias(_out_proj_body, 4, alias)
    return pl.pallas_call(
        body,
        out_shape=jax.ShapeDtypeStruct((out_rows, d), F32),
        grid=(pl.cdiv(m, tm), d // tn),
        in_specs=[
            pl.BlockSpec((tm, tok_w), lambda i, j: (i + in_blk_off, 0)),
            pl.BlockSpec((tm, mem_w), lambda i, j: (i + in_blk_off, 0)),
            pl.BlockSpec((None, tok_w + mem_w, tn), lambda i, j: (layer, 0, j)),
            pl.BlockSpec((tm, tn), lambda i, j: (i, j)),
        ] + extra_specs,
        out_specs=pl.BlockSpec((tm, tn), lambda i, j: (i + in_blk_off, j)),
        scratch_shapes=[pltpu.VMEM((tm, tok_w + mem_w), BF16)],
        compiler_params=_params("parallel", "arbitrary"),
        name="out_proj",
        **kw,
    )(tok, mem, w, x, *extra)


def _route_body(x_ref, g_ref, wh_ref, wl_ref, b_ref, info_ref, *, n_groups, per_group):
    h = _rms(x_ref[...], g_ref[...])
    hh, hl = _split_bf16(h)
    d = lambda a, b: jnp.dot(a, b, preferred_element_type=F32)
    logits = d(hh, wh_ref[...]) + (d(hh, wl_ref[...]) + d(hl, wh_ref[...])) + b_ref[...]
    lane = lax.broadcasted_iota(jnp.int32, logits.shape, 1)
    big = jnp.int32(ROUTE_LANES)

    def first_max(mask):
        v = jnp.max(jnp.where(mask, logits, NEG), axis=-1, keepdims=True)
        i = jnp.min(jnp.where(mask & (logits == v), lane, big), axis=-1, keepdims=True)
        return v, i

    g_mask = lane < n_groups
    g_max, g_sel = first_max(g_mask)
    p_group = 1.0 / jnp.sum(jnp.where(g_mask, jnp.exp(logits - g_max), 0.0), axis=-1, keepdims=True)
    lo = n_groups + g_sel * per_group
    e_mask = (lane >= lo) & (lane < lo + per_group)
    v1, i1 = first_max(e_mask)
    v2, i2 = first_max(e_mask & (lane != i1))
    e2 = jnp.exp(v2 - v1)
    w1 = p_group / (1.0 + e2)
    w2 = p_group * e2 / (1.0 + e2)
    id1 = (i1 - n_groups).astype(F32)
    id2 = (i2 - n_groups).astype(F32)
    info_ref[...] = jnp.where(lane == 0, id1, jnp.where(lane == 1, id2,
                              jnp.where(lane == 2, w1, jnp.where(lane == 3, w2, 0.0))))


def _route(x, g, w_hi, w_lo, bias, *, layer, tm, n_groups, per_group):
    m, d = x.shape
    body = functools.partial(_route_body, n_groups=n_groups, per_group=per_group)
    return pl.pallas_call(
        body,
        out_shape=jax.ShapeDtypeStruct((m, ROUTE_LANES), F32),
        grid=(pl.cdiv(m, tm),),
        in_specs=[
            pl.BlockSpec((tm, d), lambda i: (i, 0)),
            pl.BlockSpec((1, d), lambda i: (0, 0)),
            pl.BlockSpec((None, d, ROUTE_LANES), lambda i: (layer, 0, 0)),
            pl.BlockSpec((None, d, ROUTE_LANES), lambda i: (layer, 0, 0)),
            pl.BlockSpec((None, 1, ROUTE_LANES), lambda i: (layer, 0, 0)),
        ],
        out_specs=pl.BlockSpec((tm, ROUTE_LANES), lambda i: (i, 0)),
        compiler_params=_params("parallel"),
        name="moe_route",
    )(x, g, w_hi, w_lo, bias)


def _moe_plan(info, n_experts, tm):
    m = info.shape[0]
    ids = info[:, :EXPERT_TOPK].astype(jnp.int32).reshape(-1)
    wts = info[:, EXPERT_TOPK:2 * EXPERT_TOPK].reshape(-1)
    n_pairs = m * EXPERT_TOPK
    n_tiles = n_pairs // tm + n_experts
    n_slots = n_tiles * tm
    plane = m + 2 * tm
    onehot = (ids[:, None] == jnp.arange(n_experts)[None, :]).astype(jnp.int32)
    csum = jnp.cumsum(onehot, axis=0)
    rank = jnp.sum(csum * onehot, axis=1) - 1
    counts = csum[-1]
    tiles_e = (counts + tm - 1) // tm
    tile_end = jnp.cumsum(tiles_e)
    tile_start = tile_end - tiles_e
    slot_of_pair = jnp.sum(onehot * tile_start[None, :], axis=1) * tm + rank
    total = tile_end[-1]
    t_idx = jnp.arange(n_tiles, dtype=jnp.int32)
    t_eff = jnp.minimum(t_idx, total - 1).astype(jnp.int32)
    tile_expert = jnp.sum((tile_end[None, :] <= t_eff[:, None]).astype(jnp.int32), axis=1)
    tile_valid = (t_idx < total).astype(jnp.int32)
    pair = jnp.arange(n_pairs, dtype=jnp.int32)
    slot = jnp.arange(n_slots, dtype=jnp.int32)
    pad_dst = m + ((slot // tm) % 2) * tm + slot % tm
    slot_src = jnp.zeros((n_slots,), jnp.int32).at[slot_of_pair].set(pair // EXPERT_TOPK)
    slot_dst = pad_dst.at[slot_of_pair].set((pair % EXPERT_TOPK) * plane + pair // EXPERT_TOPK)
    slot_w = jnp.zeros((n_slots,), F32).at[slot_of_pair].set(wts)
    return slot_src, slot_dst, slot_w[:, None], tile_expert, t_eff, tile_valid


def _gate_up_body(te_ref, tb_ref, tv_ref, src_ref, x_hbm, g_ref, wg_ref, wu_ref, act_ref, xbuf, xn_ref, sem,
                  *, tm, n_tiles, nc):
    t, c = pl.program_id(0), pl.program_id(1)
    slot = t % 2
    rows_per_step = tm // nc

    def row_copy(tile, r, buf):
        tok = src_ref[tile * tm + r]
        return pltpu.make_async_copy(x_hbm.at[pl.ds(tok, 1)], xbuf.at[buf, pl.ds(r, 1)], sem.at[buf])

    def start_rows(tile, buf, r0, n):
        def issue(i, carry):
            row_copy(tile, r0 + i, buf).start()
            return carry
        lax.fori_loop(0, n, issue, 0, unroll=DMA_UNROLL)

    @pl.when((t == 0) & (c == 0))
    def _():
        start_rows(0, 0, 0, tm)

    nxt = jnp.minimum(t + 1, n_tiles - 1)

    @pl.when((t + 1 < n_tiles) & (tv_ref[nxt] == 1))
    def _():
        start_rows(t + 1, 1 - slot, c * rows_per_step, rows_per_step)

    @pl.when(tv_ref[t] == 1)
    def _():
        @pl.when(c == 0)
        def _():
            pltpu.make_async_copy(x_hbm.at[pl.ds(0, tm)], xbuf.at[slot], sem.at[slot]).wait()
            xn_ref[...] = _rms(xbuf[slot], g_ref[...]).astype(BF16)

        x = xn_ref[...]
        gate = jnp.dot(x, wg_ref[...].astype(BF16), preferred_element_type=F32)
        up = jnp.dot(x, wu_ref[...].astype(BF16), preferred_element_type=F32)
        act_ref[...] = (gate * jax.nn.sigmoid(gate) * up).astype(BF16)


def _gate_up(x, g, w_gate_up, slot_src, tile_expert, tile_block, tile_valid, *, layer, tm, fc):
    d = x.shape[1]
    f = w_gate_up.shape[3] // 2
    nc = f // fc
    n_tiles = tile_expert.shape[0]
    cc = lambda t, c, tv: jnp.where(tv[t] == 1, c, nc - 1)
    body = functools.partial(_gate_up_body, tm=tm, n_tiles=n_tiles, nc=nc)
    grid_spec = pltpu.PrefetchScalarGridSpec(
        num_scalar_prefetch=4,
        grid=(n_tiles, nc),
        in_specs=[
            pl.BlockSpec(memory_space=pl.ANY),
            pl.BlockSpec((1, d), lambda t, c, te, tb, tv, src: (0, 0)),
            pl.BlockSpec((None, None, d, fc), lambda t, c, te, tb, tv, src: (layer, te[t], 0, cc(t, c, tv))),
            pl.BlockSpec((None, None, d, fc), lambda t, c, te, tb, tv, src: (layer, te[t], 0, nc + cc(t, c, tv))),
        ],
        out_specs=pl.BlockSpec((tm, fc), lambda t, c, te, tb, tv, src: (tb[t], cc(t, c, tv))),
        scratch_shapes=[pltpu.VMEM((2, tm, d), F32), pltpu.VMEM((tm, d), BF16), pltpu.SemaphoreType.DMA((2,))],
    )
    return pl.pallas_call(
        body,
        out_shape=jax.ShapeDtypeStruct((n_tiles * tm, f), BF16),
        grid_spec=grid_spec,
        compiler_params=_params("arbitrary", "arbitrary"),
        name="moe_gate_up",
    )(tile_expert, tile_block, tile_valid, slot_src, x, g, w_gate_up, w_gate_up)


def _down_body(te_ref, tb_ref, tv_ref, dst_ref, act_ref, wd_ref, sw_ref, ys_hbm, obuf, sem, *, tm, n_tiles, nj):
    t, j = pl.program_id(0), pl.program_id(1)
    slot = t % 2
    tn = wd_ref.shape[1]

    def wait_rows(buf):
        pltpu.make_async_copy(obuf.at[buf], ys_hbm.at[pl.ds(0, tm)], sem.at[buf]).wait()

    def start_rows(tile, buf):
        def issue(r, carry):
            dst = dst_ref[tile * tm + r]
            pltpu.make_async_copy(obuf.at[buf, pl.ds(r, 1)], ys_hbm.at[pl.ds(dst, 1)], sem.at[buf]).start()
            return carry
        lax.fori_loop(0, tm, issue, 0, unroll=DMA_UNROLL)

    @pl.when((j == 0) & (t >= 2) & (tv_ref[jnp.maximum(t - 2, 0)] == 1))
    def _():
        wait_rows(slot)

    @pl.when(tv_ref[t] == 1)
    def _():
        y = jnp.dot(act_ref[...], wd_ref[...].astype(BF16), preferred_element_type=F32) * sw_ref[...]
        for jj in range(nj):
            @pl.when(j == jj)
            def _():
                obuf[slot, :, jj * tn:(jj + 1) * tn] = y

        @pl.when(j == nj - 1)
        def _():
            start_rows(t, slot)

    @pl.when((t == n_tiles - 1) & (j == nj - 1))
    def _():
        @pl.when((t >= 1) & (tv_ref[jnp.maximum(t - 1, 0)] == 1))
        def _():
            wait_rows(1 - slot)

        @pl.when(tv_ref[t] == 1)
        def _():
            wait_rows(slot)


def _down_scatter(act, w_down, slot_dst, slot_w, tile_expert, tile_block, tile_valid, *, layer, tm, tn, out_rows):
    f, d = w_down.shape[2:]
    nj = d // tn
    n_tiles = tile_expert.shape[0]
    jc = lambda t, j, tv: jnp.where(tv[t] == 1, j, nj - 1)
    body = functools.partial(_down_body, tm=tm, n_tiles=n_tiles, nj=nj)
    grid_spec = pltpu.PrefetchScalarGridSpec(
        num_scalar_prefetch=4,
        grid=(n_tiles, nj),
        in_specs=[
            pl.BlockSpec((tm, f), lambda t, j, te, tb, tv, dst: (tb[t], 0)),
            pl.BlockSpec((None, None, f, tn), lambda t, j, te, tb, tv, dst: (layer, te[t], 0, jc(t, j, tv))),
            pl.BlockSpec((tm, 1), lambda t, j, te, tb, tv, dst: (tb[t], 0)),
        ],
        out_specs=pl.BlockSpec(memory_space=pl.ANY),
        scratch_shapes=[pltpu.VMEM((2, tm, d), F32), pltpu.SemaphoreType.DMA((2,))],
    )
    return pl.pallas_call(
        body,
        out_shape=jax.ShapeDtypeStruct((out_rows, d), F32),
        grid_spec=grid_spec,
        compiler_params=_params("arbitrary", "arbitrary"),
        name="moe_down_scatter",
    )(tile_expert, tile_block, tile_valid, slot_dst, act, w_down, slot_w)


def _combine_body(x_ref, y0_ref, y1_ref, o_ref):
    o_ref[...] = x_ref[...] + y0_ref[...] + y1_ref[...]


def _combine(x, ys, *, tm, row_blk_off=0, n_rows=None):
    d = x.shape[1]
    n_rows = x.shape[0] if n_rows is None else n_rows
    return pl.pallas_call(
        _combine_body,
        out_shape=jax.ShapeDtypeStruct((n_rows, d), F32),
        grid=(pl.cdiv(n_rows, tm),),
        in_specs=[
            pl.BlockSpec((tm, d), lambda i: (i + row_blk_off, 0)),
            pl.BlockSpec((None, tm, d), lambda i: (0, i + row_blk_off, 0)),
            pl.BlockSpec((None, tm, d), lambda i: (1, i + row_blk_off, 0)),
        ],
        out_specs=pl.BlockSpec((tm, d), lambda i: (i, 0)),
        compiler_params=_params("parallel"),
        name="moe_combine",
    )(x, ys, ys)


def _moe_parts(x, g, w_rt_hi, w_rt_lo, b_rt, w_gate_up, w_down, *, layer, n_groups, per_group, tm_route, tm, fc, tn):
    m, d = x.shape
    n_experts = n_groups * per_group
    info = _route(x, g, w_rt_hi, w_rt_lo, b_rt, layer=layer, tm=tm_route, n_groups=n_groups, per_group=per_group)
    slot_src, slot_dst, slot_w, te, tb, tv = _moe_plan(info, n_experts, tm)
    act = _gate_up(x, g, w_gate_up, slot_src, te, tb, tv, layer=layer, tm=tm, fc=fc)
    plane = m + 2 * tm
    ys = _down_scatter(act, w_down, slot_dst, slot_w, te, tb, tv, layer=layer, tm=tm, tn=tn,
                       out_rows=EXPERT_TOPK * plane)
    return ys.reshape(EXPERT_TOPK, plane, d)


def _moba_prompt_body(slopes_ref, q_ref, gq_ref, k_ref, v_ref, o_ref, *, kv_group, blk):
    g, own = pl.program_id(1), pl.program_id(2)
    hd = k_ref.shape[1]
    nb = k_ref.shape[0] // blk
    rows = kv_group * blk
    q = q_ref[...]
    qn = jnp.concatenate([_rms(q[:, r * hd:(r + 1) * hd], gq_ref[...]) for r in range(kv_group)], axis=0)
    qb = qn.astype(BF16)
    row = lax.broadcasted_iota(jnp.int32, (rows, 1), 0)
    slope = jnp.zeros((rows, 1), F32)
    for r in range(kv_group):
        slope = jnp.where(row // blk == r, slopes_ref[g * kv_group + r], slope)
    trow = row % blk
    col = lax.broadcasted_iota(jnp.int32, (1, blk), 1)
    rel = slope * (trow - col).astype(F32)
    scale = hd ** -0.5

    nbp = -(-nb // 8) * 8
    km = jnp.concatenate([jnp.mean(k_ref[n * blk:(n + 1) * blk, :], axis=0, keepdims=True) for n in range(nb)]
                         + [jnp.zeros((1, hd), F32)] * (nbp - nb), axis=0)
    blk_id = lax.broadcasted_iota(jnp.int32, (nbp, rows), 0)
    gate = jnp.where(blk_id < own, _dot3(km, qn, _NT), NEG)
    cnt = jnp.zeros(gate.shape, jnp.int32)
    for m_ in range(nb):
        gm = gate[m_:m_ + 1, :]
        ahead = jnp.where(gm > gate, 1, jnp.where(gm == gate, jnp.where(m_ < blk_id, 1, 0), 0))
        cnt = cnt + ahead
    skip = jnp.where(cnt < MOBA_TOPK, jnp.where(blk_id < own, 0.0, -NEG), -NEG)
    skip = jnp.concatenate([skip, jnp.zeros((ROUTE_LANES - nbp, rows), F32)], axis=0).T
    lane = lax.broadcasted_iota(jnp.int32, (rows, ROUTE_LANES), 1)

    def block_scores(n):
        start = pl.multiple_of(n * blk, blk)
        kb = k_ref[pl.ds(start, blk), :].astype(BF16)
        vb = v_ref[pl.ds(start, blk), :].astype(BF16)
        s = lax.dot_general(qb, kb, _NT, preferred_element_type=F32) * scale
        return s - rel, vb

    s, vb = block_scores(own)
    s = jnp.where(col <= trow, s, NEG)
    m0 = jnp.max(s, axis=-1, keepdims=True)
    p = jnp.exp(s - m0)
    l0 = jnp.sum(p, axis=-1, keepdims=True)
    a0 = jnp.dot(p.astype(BF16), vb, preferred_element_type=F32)

    def step(n, carry):
        m_run, l_run, acc = carry
        s, vb = block_scores(n)
        far = slope * ((own - n) * blk).astype(F32)
        s = s - (far + jnp.sum(jnp.where(lane == n, skip, 0.0), axis=-1, keepdims=True))
        m_new = jnp.maximum(m_run, jnp.max(s, axis=-1, keepdims=True))
        alpha = jnp.exp(m_run - m_new)
        p = jnp.exp(s - m_new)
        l_new = alpha * l_run + jnp.sum(p, axis=-1, keepdims=True)
        acc = alpha * acc + jnp.dot(p.astype(BF16), vb, preferred_element_type=F32)
        return m_new, l_new, acc

    _, l_fin, acc = lax.fori_loop(0, own, step, (m0, l0, a0))
    out = acc / l_fin
    for r in range(kv_group):
        o_ref[:, r * hd:(r + 1) * hd] = out[r * blk:(r + 1) * blk]


def _moba_prompt(z, gq, kv, slopes, *, n_batch, t_len, n_kv, kv_group, out_rows):
    hd = gq.shape[1]
    blk = MOBA_BLOCK
    gw = kv_group * hd
    nt = t_len // blk
    body = functools.partial(_moba_prompt_body, kv_group=kv_group, blk=blk)
    grid_spec = pltpu.PrefetchScalarGridSpec(
        num_scalar_prefetch=1,
        grid=(n_batch, n_kv, nt),
        in_specs=[
            pl.BlockSpec((blk, gw), lambda b, g, t, s: (b * nt + t, g)),
            pl.BlockSpec((1, hd), lambda b, g, t, s: (0, 0)),
            pl.BlockSpec((t_len, hd), lambda b, g, t, s: (b, g)),
            pl.BlockSpec((t_len, hd), lambda b, g, t, s: (b, n_kv + g)),
        ],
        out_specs=pl.BlockSpec((blk, gw), lambda b, g, t, s: (b * nt + t, g)),
    )
    return pl.pallas_call(
        body,
        out_shape=jax.ShapeDtypeStruct((out_rows, n_kv * gw), F32),
        grid_spec=grid_spec,
        compiler_params=_params("parallel", "parallel", "arbitrary"),
        name="moba_prompt",
    )(slopes, z, gq, kv, kv)


def _sample_queries(q, gq, n_heads, hd):
    return jnp.concatenate([_rms(q[:, h * hd:(h + 1) * hd], gq) for h in range(n_heads)], axis=0)


def _row_slopes(slopes_ref, n_heads, t_len):
    row = lax.broadcasted_iota(jnp.int32, (n_heads * t_len, 1), 0)
    slope = jnp.zeros((n_heads * t_len, 1), F32)
    for h in range(n_heads):
        slope = jnp.where(row // t_len == h, slopes_ref[h], slope)
    return slope


def _moba_partial_body(pt_ref, slopes_ref, q_ref, gq_ref, k0_ref, k1_ref, v0_ref, v1_ref, po_ref, ps_ref,
                       qn_ref, bias_ref, *, kv_group, past_len):
    n = pl.program_id(1)
    page, n_kv, hd = k0_ref.shape
    t_len = q_ref.shape[0]
    n_heads = n_kv * kv_group
    rows = n_heads * t_len
    blk = 2 * page
    cols = blk * n_kv
    row = lax.broadcasted_iota(jnp.int32, (rows, 1), 0)

    @pl.when(n == 0)
    def _():
        qn_ref[...] = _sample_queries(q_ref[...], gq_ref[...], n_heads, hd)
        col = lax.broadcasted_iota(jnp.int32, (1, cols), 1)
        same_head = row // (kv_group * t_len) == col % n_kv
        slope = _row_slopes(slopes_ref, n_heads, t_len)
        bias_ref[...] = jnp.where(same_head, slope * (past_len + row % t_len - col // n_kv).astype(F32), -NEG)

    qn = qn_ref[...]
    slope = _row_slopes(slopes_ref, n_heads, t_len)
    kb = jnp.concatenate([k0_ref[...].reshape(page * n_kv, hd), k1_ref[...].reshape(page * n_kv, hd)], axis=0)
    vb = jnp.concatenate([v0_ref[...].reshape(page * n_kv, hd), v1_ref[...].reshape(page * n_kv, hd)], axis=0)
    s = lax.dot_general(qn.astype(BF16), kb.astype(BF16), _NT, preferred_element_type=F32) * (hd ** -0.5)
    s = s - (bias_ref[...] - slope * (n * blk).astype(F32))
    mx = jnp.max(s, axis=-1, keepdims=True)
    p = jnp.exp(s - mx)
    den = jnp.sum(p, axis=-1, keepdims=True)
    po_ref[...] = jnp.dot(p.astype(BF16), vb.astype(BF16), preferred_element_type=F32)
    km = (jnp.sum(k0_ref[...], axis=0) + jnp.sum(k1_ref[...], axis=0)) / blk
    km_rows = jnp.concatenate([jnp.broadcast_to(km[g:g + 1], (kv_group * t_len, hd)) for g in range(n_kv)], axis=0)
    gate = jnp.sum(qn * km_rows, axis=-1, keepdims=True)
    lane = lax.broadcasted_iota(jnp.int32, (rows, ROUTE_LANES), 1)
    ps_ref[...] = jnp.where(lane == 0, mx, jnp.where(lane == 1, den, jnp.where(lane == 2, gate, 0.0)))


def _moba_partials(z, gq, cache_k, cache_v, page_table, slopes, *, n_batch, t_len, row_off, kv_group, past_len):
    page, n_kv, hd = cache_k.shape[1:]
    pages_per_blk = MOBA_BLOCK // page
    assert pages_per_blk == 2
    n_pages = page_table.shape[1]
    nblk = n_pages // pages_per_blk
    n_heads = n_kv * kv_group
    rows = n_heads * t_len
    body = functools.partial(_moba_partial_body, kv_group=kv_group, past_len=past_len)
    page_spec = lambda which: pl.BlockSpec(
        (None, page, n_kv, hd), lambda b, n, pt, s: (pt[b * n_pages + n * pages_per_blk + which], 0, 0, 0))
    grid_spec = pltpu.PrefetchScalarGridSpec(
        num_scalar_prefetch=2,
        grid=(n_batch, nblk),
        in_specs=[
            pl.BlockSpec((t_len, n_heads * hd), lambda b, n, pt, s: (row_off // t_len + b, 0)),
            pl.BlockSpec((1, hd), lambda b, n, pt, s: (0, 0)),
            page_spec(0), page_spec(1), page_spec(0), page_spec(1),
        ],
        out_specs=(
            pl.BlockSpec((None, None, rows, hd), lambda b, n, pt, s: (b, n, 0, 0)),
            pl.BlockSpec((None, None, rows, ROUTE_LANES), lambda b, n, pt, s: (b, n, 0, 0)),
        ),
        scratch_shapes=[pltpu.VMEM((rows, hd), F32), pltpu.VMEM((rows, MOBA_BLOCK * n_kv), F32)],
    )
    return pl.pallas_call(
        body,
        out_shape=(jax.ShapeDtypeStruct((n_batch, nblk, rows, hd), F32),
                   jax.ShapeDtypeStruct((n_batch, nblk, rows, ROUTE_LANES), F32)),
        grid_spec=grid_spec,
        compiler_params=_params("parallel", "arbitrary"),
        name="moba_partials",
    )(page_table.reshape(-1), slopes, z, gq, cache_k, cache_k, cache_v, cache_v)


def _moba_merge_body(slopes_ref, q_ref, gq_ref, kn_ref, vn_ref, po_ref, ps_ref, o_ref, *, n_kv, kv_group):
    hd = gq_ref.shape[1]
    t_len = q_ref.shape[0]
    n_heads = n_kv * kv_group
    rows = n_heads * t_len
    grows = kv_group * t_len
    nblk = ps_ref.shape[0]
    lane = lax.broadcasted_iota(jnp.int32, (rows, ROUTE_LANES), 1)
    m_blk = jnp.full((rows, ROUTE_LANES), NEG, F32)
    l_blk = jnp.zeros((rows, ROUTE_LANES), F32)
    gates = jnp.full((rows, ROUTE_LANES), NEG, F32)
    for n in range(nblk):
        st = ps_ref[n]
        m_blk = jnp.where(lane == n, st[:, 0:1], m_blk)
        l_blk = jnp.where(lane == n, st[:, 1:2], l_blk)
        gates = jnp.where(lane == n, st[:, 2:3], gates)
    w_blk = jnp.zeros((rows, ROUTE_LANES), F32)
    for _ in range(min(MOBA_TOPK, nblk)):
        best = jnp.max(gates, axis=-1, keepdims=True)
        first = jnp.min(jnp.where(gates == best, lane, ROUTE_LANES), axis=-1, keepdims=True)
        w_blk = jnp.where(lane == first, 1.0, w_blk)
        gates = jnp.where(lane == first, NEG, gates)
    m_blk = jnp.where(w_blk > 0.0, m_blk, NEG)
    m_past = jnp.max(m_blk, axis=-1, keepdims=True)

    qn = _sample_queries(q_ref[...], gq_ref[...], n_heads, hd)
    slope = _row_slopes(slopes_ref, n_heads, t_len)
    row = lax.broadcasted_iota(jnp.int32, (rows, 1), 0)
    tq = row % t_len
    tk = lax.broadcasted_iota(jnp.int32, (1, t_len), 1)
    s_own = jnp.concatenate([
        lax.dot_general(qn[g * grows:(g + 1) * grows].astype(BF16), kn_ref[:, g * hd:(g + 1) * hd].astype(BF16), _NT,
                        preferred_element_type=F32) for g in range(n_kv)], axis=0) * (hd ** -0.5)
    s_own = jnp.where(tk <= tq, s_own - slope * (tq - tk).astype(F32), NEG)
    m_all = jnp.maximum(m_past, jnp.max(s_own, axis=-1, keepdims=True))
    w_blk = jnp.where(w_blk > 0.0, jnp.exp(m_blk - m_all), 0.0)
    p_own = jnp.exp(s_own - m_all)
    den = jnp.sum(p_own, axis=-1, keepdims=True) + jnp.sum(w_blk * l_blk, axis=-1, keepdims=True)
    num = jnp.zeros((rows, hd), F32)
    for n in range(nblk):
        num = num + w_blk[:, n:n + 1] * po_ref[n]
    num = num + jnp.concatenate([
        jnp.dot(p_own[g * grows:(g + 1) * grows].astype(BF16), vn_ref[:, g * hd:(g + 1) * hd].astype(BF16),
                preferred_element_type=F32) for g in range(n_kv)], axis=0)
    out = num / den
    for h in range(n_heads):
        o_ref[:, h * hd:(h + 1) * hd] = out[h * t_len:(h + 1) * t_len]


def _moba_merge(z, gq, kv, part_o, part_s, slopes, *, n_batch, t_len, row_off, n_kv, kv_group, alias):
    hd = gq.shape[1]
    kw = n_kv * hd
    nblk, rows = part_o.shape[1:3]
    body = functools.partial(_moba_merge_body, n_kv=n_kv, kv_group=kv_group)
    wrapped = lambda *refs: body(*refs[:7], *refs[8:])
    new_rows = lambda b, s: (row_off // t_len + b, 0)
    grid_spec = pltpu.PrefetchScalarGridSpec(
        num_scalar_prefetch=1,
        grid=(n_batch,),
        in_specs=[
            pl.BlockSpec((t_len, kv_group * kw), new_rows),
            pl.BlockSpec((1, hd), lambda b, s: (0, 0)),
            pl.BlockSpec((t_len, kw), new_rows),
            pl.BlockSpec((t_len, kw), lambda b, s: (row_off // t_len + b, 1)),
            pl.BlockSpec((None, nblk, rows, hd), lambda b, s: (b, 0, 0, 0)),
            pl.BlockSpec((None, nblk, rows, ROUTE_LANES), lambda b, s: (b, 0, 0, 0)),
            pl.BlockSpec(memory_space=pl.ANY),
        ],
        out_specs=pl.BlockSpec((t_len, kv_group * kw), new_rows),
    )
    return pl.pallas_call(
        wrapped,
        out_shape=jax.ShapeDtypeStruct(alias.shape, F32),
        grid_spec=grid_spec,
        input_output_aliases={7: 0},
        compiler_params=_params("parallel"),
        name="moba_merge",
    )(slopes, z, gq, kv, kv, part_o, part_s, alias)


def _tile(m, pref):
    return pref if m >= pref else m


def kernel(x_prompt, x_sample, mem_prompt, cache_k, cache_v, cache_mem_k, cache_mem_v, state_pool, page_table, g_mix_norm, w_in, w_out, w_pool, pool_scale, g_q, g_kv_norm, w_kv, g_k, g_mem_norm, w_mem_kv, g_mem_q, g_mem_k, g_ffn_norm, w_group, b_group, w_router, b_router, w_gate_up, w_down):
    bp, tp, d = x_prompt.shape
    bs, ts, _ = x_sample.shape
    depth = w_in.shape[0]
    n_a = w_pool.shape[0]
    mem_tokens = mem_prompt.shape[1]
    mem_w = d // 4
    mem_hd = mem_w // MEM_HEADS
    tok_w = d - mem_w
    page, n_kv, hd = cache_k.shape[1:]
    kv_w = n_kv * hd
    n_heads = tok_w // hd
    kv_group = n_heads // n_kv
    past_len = page_table.shape[1] * page
    n_groups = w_group.shape[2]
    n_experts = w_router.shape[2]
    per_group = n_experts // n_groups
    d_expert = w_down.shape[2]
    mp, ms = bp * tp, bs * ts
    m_all = mp + ms

    tm_p = _tile(mp, 512)
    tn = _tile(d, 1024)
    tt_p = _tile(tp, MOBA_BLOCK)
    fc = _tile(d_expert, 256)
    tn_e = _tile(d, 2048)
    tm_e = 512 if m_all >= 4096 else 128
    tm_c = _tile(mp, 256)

    row = lambda v: v.reshape(1, -1).astype(F32)
    w_in_b, w_out_b, w_pool_b = w_in.astype(BF16), w_out.astype(BF16), w_pool.astype(BF16)
    w_kv_b, w_mem_kv_b = w_kv.astype(BF16)[None], w_mem_kv.astype(BF16)
    w_rt = jnp.concatenate([w_group, w_router, jnp.zeros((depth, d, ROUTE_LANES - n_groups - n_experts), F32)], axis=-1)
    w_rt_hi = w_rt.astype(BF16)
    w_rt_lo = (w_rt - w_rt_hi.astype(F32)).astype(BF16)
    b_rt = jnp.concatenate([b_group, b_router, jnp.zeros((depth, ROUTE_LANES - n_groups - n_experts), F32)],
                           axis=-1)[:, None, :]
    slopes = jnp.exp2(-8.0 * jnp.arange(1, n_heads + 1, dtype=F32) / n_heads)

    xp = x_prompt.reshape(mp, d)
    xs_ = x_sample.reshape(ms, d)
    sample_blk = mp // ms

    memx = mem_prompt.reshape(bp * mem_tokens, d)
    mem_kv_p = [
        _norm_matmul(memx, row(g_mem_norm[l]), w_mem_kv_b, layer=l, n_out=2 * mem_w, tm=_tile(bp * mem_tokens, 512),
                     tn=mem_w, head_groups=(mem_hd, 0),
                     gh=jnp.concatenate([jnp.tile(row(g_mem_k[l]), (1, MEM_HEADS)), jnp.ones((1, mem_w), F32)], axis=1))
        for l in range(depth)]
    mem_kv_p3 = [a.reshape(bp, mem_tokens, 2 * mem_w) for a in mem_kv_p]
    cmk = cache_mem_k.reshape(depth, bs, mem_tokens, mem_w)
    cmv = cache_mem_v.reshape(depth, bs, mem_tokens, mem_w)

    def mem_attend(z, l):
        y = _mem_attn(z, row(g_mem_q[l]),
                      mem_kv_p3[l], pl.BlockSpec((None, mem_tokens, mem_w), lambda b, t: (b, 0, 0)),
                      mem_kv_p3[l], pl.BlockSpec((None, mem_tokens, mem_w), lambda b, t: (b, 0, 1)),
                      n_batch=bp, t_len=tp, tt=tt_p, row_off=0, out_rows=m_all)
        return _mem_attn(z, row(g_mem_q[l]),
                         cmk, pl.BlockSpec((None, None, mem_tokens, mem_w), lambda b, t: (l, b, 0, 0)),
                         cmv, pl.BlockSpec((None, None, mem_tokens, mem_w), lambda b, t: (l, b, 0, 0)),
                         n_batch=bs, t_len=ts, tt=ts, row_off=mp, out_rows=m_all, alias=y)

    def moe_parts(x, l):
        return _moe_parts(x, row(g_ffn_norm[l]), w_rt_hi, w_rt_lo, b_rt, w_gate_up, w_down, layer=l,
                          n_groups=n_groups, per_group=per_group, tm_route=tm_p, tm=tm_e, fc=fc, tn=tn_e)

    x = None
    z_pool = []
    kv = None
    y_prompt = y_sample = None
    for l in range(depth):
        g_mix = row(g_mix_norm[l])
        if l == 0:
            z = _norm_matmul(xp, g_mix, w_in_b, layer=l, n_out=d, tm=tm_p, tn=tn, out_rows=m_all)
            z = _norm_matmul(xs_, g_mix, w_in_b, layer=l, n_out=d, tm=ms, tn=tn, out_rows=m_all,
                             row_blk_off=sample_blk, alias=z)
        else:
            z = _norm_matmul(x, g_mix, w_in_b, layer=l, n_out=d, tm=tm_p, tn=tn)
        mem_y = mem_attend(z, l)
        if l < n_a:
            z_pool.append(z)
            scale = row(pool_scale[l])
            zero_buf = jnp.zeros((bp, POOL_HALO, tok_w), F32)
            samp_buf = jnp.concatenate([jnp.zeros((bs, 1, tok_w), F32), state_pool[l]], axis=1)
            tok_y = _pool_mix(z, zero_buf, w_pool_b[l], scale, n_batch=bp, t_len=tp, tt=tt_p, row_off=0, pos0=0,
                              out_rows=m_all)
            tok_y = _pool_mix(z, samp_buf, w_pool_b[l], scale, n_batch=bs, t_len=ts, tt=ts, row_off=mp,
                              pos0=past_len, out_rows=m_all, alias=tok_y)
        else:
            if kv is None:
                kv = _norm_matmul(x, row(g_kv_norm), w_kv_b, layer=0, n_out=2 * kv_w, tm=tm_p, tn=kv_w,
                                  head_groups=(hd, 0),
                                  gh=jnp.concatenate([jnp.tile(row(g_k), (1, n_kv)), jnp.ones((1, kv_w), F32)], axis=1))
            gq = row(g_q[l - n_a])
            tok_y = _moba_prompt(z, gq, kv, slopes, n_batch=bp, t_len=tp, n_kv=n_kv, kv_group=kv_group,
                                 out_rows=m_all)
            part_o, part_s = _moba_partials(z, gq, cache_k, cache_v, page_table, slopes, n_batch=bs, t_len=ts,
                                            row_off=mp, kv_group=kv_group, past_len=past_len)
            tok_y = _moba_merge(z, gq, kv, part_o, part_s, slopes, n_batch=bs, t_len=ts, row_off=mp,
                                n_kv=n_kv, kv_group=kv_group, alias=tok_y)
        if l == 0:
            x = _out_proj(tok_y, mem_y, w_out_b, xp, layer=l, tm=tm_p, tn=tn, out_rows=m_all)
            x = _out_proj(tok_y, mem_y, w_out_b, xs_, layer=l, tm=ms, tn=tn, in_blk_off=sample_blk, out_rows=m_all,
                          alias=x)
        else:
            x = _out_proj(tok_y, mem_y, w_out_b, x, layer=l, tm=tm_p, tn=tn)
        ys = moe_parts(x, l)
        if l + 1 < depth:
            x = _combine(x, ys, tm=tm_c)
        else:
            y_prompt = _combine(x, ys, tm=tm_c, n_rows=mp)
            y_sample = _combine(x, ys, tm=ms, row_blk_off=sample_blk, n_rows=ms)

    y_prompt = y_prompt.reshape(bp, tp, d)
    y_sample = y_sample.reshape(bs, ts, d)
    k_prompt = kv[:mp, :kv_w].reshape(bp, tp, n_kv, hd)
    v_prompt = kv[:mp, kv_w:].reshape(bp, tp, n_kv, hd)
    k_sample = kv[mp:, :kv_w].reshape(bs, ts, n_kv, hd)
    v_sample = kv[mp:, kv_w:].reshape(bs, ts, n_kv, hd)
    mem_k_prompt = jnp.stack([a[:, :, :mem_w].reshape(bp, mem_tokens, MEM_HEADS, mem_hd) for a in mem_kv_p3])
    mem_v_prompt = jnp.stack([a[:, :, mem_w:].reshape(bp, mem_tokens, MEM_HEADS, mem_hd) for a in mem_kv_p3])
    pool_prompt = jnp.stack([zz[:mp].reshape(bp, tp, d)[:, tp - POOL_BUF:, :tok_w] for zz in z_pool])
    pool_sample = jnp.stack([
        jnp.concatenate([state_pool[i], zz[mp:].reshape(bs, ts, d)[:, :, :tok_w]], axis=1)[:, -POOL_BUF:]
        for i, zz in enumerate(z_pool)])
    return (y_prompt, y_sample, k_prompt, v_prompt, k_sample, v_sample,
            mem_k_prompt, mem_v_prompt, pool_prompt, pool_sample)
```

```python
import functools

import jax
import jax.numpy as jnp
from jax import lax
from jax.experimental import pallas as pl
from jax.experimental.pallas import tpu as pltpu

F32 = jnp.float32
BF16 = jnp.bfloat16

EPS = 1e-6
POOL_WINDOWS = (2, 4, 8, 16)
POOL_BUF = max(POOL_WINDOWS) - 1
POOL_HALO = POOL_BUF + 1
MEM_HEADS = 4
MOBA_BLOCK = 256
MOBA_TOPK = 3
EXPERT_TOPK = 2
ROUTE_LANES = 128
NEG = -1e30
VMEM_LIMIT = 56 * 1024 * 1024
DMA_UNROLL = 8

_NT = (((1,), (1,)), ((), ()))
_NN = (((1,), (0,)), ((), ()))


def _params(*sem, gather=False):
    return pltpu.CompilerParams(dimension_semantics=sem, vmem_limit_bytes=VMEM_LIMIT, disable_bounds_checks=gather)


def _split_bf16(a):
    hi = a.astype(BF16)
    lo = (a - hi.astype(F32)).astype(BF16)
    return hi, lo


def _dot3(a, b, dims):
    ah, al = _split_bf16(a)
    bh, bl = _split_bf16(b)
    d = lambda x, y: lax.dot_general(x, y, dims, preferred_element_type=F32)
    return d(ah, bh) + (d(ah, bl) + d(al, bh))


def _rms(x, g):
    return x * lax.rsqrt(jnp.mean(x * x, axis=-1, keepdims=True) + EPS) * g


def _matmul(a, w):
    if w.dtype == F32:
        return _dot3(a.astype(F32), w, _NN)
    return jnp.dot(a.astype(BF16), w, preferred_element_type=F32)


def _with_alias(body, n_in, alias):
    if alias is None:
        return body, [], [], {}
    wrapped = lambda *refs: body(*refs[:n_in], *refs[n_in + 1:])
    return wrapped, [alias], [pl.BlockSpec(memory_space=pl.ANY)], {"input_output_aliases": {n_in: 0}}


def _norm_matmul_body(x_ref, g_ref, w_ref, gh_ref, o_ref, hn_ref, *, head_groups):
    j = pl.program_id(1)

    @pl.when(j == 0)
    def _():
        hn_ref[...] = _rms(x_ref[...], g_ref[...]).astype(hn_ref.dtype)

    acc = _matmul(hn_ref[...], w_ref[...])
    tn = acc.shape[1]

    def write(gs):
        if gs == 0:
            o_ref[...] = acc
            return
        for c in range(tn // gs):
            sl = slice(c * gs, (c + 1) * gs)
            o_ref[:, sl] = _rms(acc[:, sl], gh_ref[:, sl])

    if len(set(head_groups)) == 1:
        write(head_groups[0])
    else:
        for jj, gs in enumerate(head_groups):
            pl.when(j == jj)(functools.partial(write, gs))


def _norm_matmul(x, g, w, *, layer, n_out, tm, tn, head_groups=None, gh=None,
                 out_rows=None, row_blk_off=0, alias=None):
    m, k = x.shape
    nj = n_out // tn
    head_groups = tuple(head_groups) if head_groups else (0,) * nj
    if gh is None:
        gh = jnp.ones((1, n_out), F32)
    out_rows = m if out_rows is None else out_rows
    body = functools.partial(_norm_matmul_body, head_groups=head_groups)
    body, extra, extra_specs, kw = _with_alias(body, 4, alias)
    return pl.pallas_call(
        body,
        out_shape=jax.ShapeDtypeStruct((out_rows, n_out), F32),
        grid=(pl.cdiv(m, tm), nj),
        in_specs=[
            pl.BlockSpec((tm, k), lambda i, j: (i, 0)),
            pl.BlockSpec((1, k), lambda i, j: (0, 0)),
            pl.BlockSpec((None, k, tn), lambda i, j: (layer, 0, j)),
            pl.BlockSpec((1, tn), lambda i, j: (0, j)),
        ] + extra_specs,
        out_specs=pl.BlockSpec((tm, tn), lambda i, j: (i + row_blk_off, j)),
        scratch_shapes=[pltpu.VMEM((tm, k), w.dtype)],
        compiler_params=_params("parallel", "arbitrary"),
        name="norm_matmul",
        **kw,
    )(x, g, w, gh, *extra)


def _pool_body(cur_ref, halo_ref, buf_ref, w_ref, scale_ref, o_ref, *, tt, pos0):
    t = pl.program_id(1)
    cur = cur_ref[...]
    halo = jnp.where(t == 0, buf_ref[...], halo_ref[...])
    full = jnp.concatenate([halo, cur], axis=0)
    pos = pos0 + t * tt + lax.broadcasted_iota(jnp.int32, (tt, 1), 0)
    gw = cur.shape[1] // len(POOL_WINDOWS)
    for gi, win in enumerate(POOL_WINDOWS):
        sl = slice(gi * gw, (gi + 1) * gw)
        s = full[:, sl]
        sh = 1
        while sh < win:
            s = s + pltpu.roll(s, shift=sh, axis=0)
            sh *= 2
        cnt = jnp.minimum(pos + 1, win).astype(F32)
        d = s[POOL_HALO:] / cnt - cur[:, sl]
        y = _matmul(d, w_ref[gi])
        o_ref[:, sl] = y * scale_ref[:, sl]


def _pool_mix(z, buf16, w_pool, scale, *, n_batch, t_len, tt, row_off, pos0, out_rows, alias=None):
    tok_w = scale.shape[1]
    nt = t_len // tt
    cur_blk = lambda b, t: ((row_off + b * t_len) // tt + t, 0)
    halo_blk = lambda b, t: (jnp.maximum((row_off + b * t_len + t * tt) // POOL_HALO - 1, 0), 0)
    body = functools.partial(_pool_body, tt=tt, pos0=pos0)
    body, extra, extra_specs, kw = _with_alias(body, 5, alias)
    return pl.pallas_call(
        body,
        out_shape=jax.ShapeDtypeStruct((out_rows, tok_w), F32),
        grid=(n_batch, nt),
        in_specs=[
            pl.BlockSpec((tt, tok_w), cur_blk),
            pl.BlockSpec((POOL_HALO, tok_w), halo_blk),
            pl.BlockSpec((None, POOL_HALO, tok_w), lambda b, t: (b, 0, 0)),
            pl.BlockSpec(w_pool.shape, lambda b, t: (0, 0, 0)),
            pl.BlockSpec((1, tok_w), lambda b, t: (0, 0)),
        ] + extra_specs,
        out_specs=pl.BlockSpec((tt, tok_w), cur_blk),
        compiler_params=_params("parallel", "arbitrary"),
        name="pool_mix",
        **kw,
    )(z, z, buf16, w_pool, scale, *extra)


def _mem_attn_body(q_ref, g_ref, k_ref, v_ref, o_ref):
    q = q_ref[...]
    hd = q.shape[1] // MEM_HEADS
    for h in range(MEM_HEADS):
        sl = slice(h * hd, (h + 1) * hd)
        qh = _rms(q[:, sl], g_ref[...]).astype(BF16)
        s = lax.dot_general(qh, k_ref[:, sl].astype(BF16), _NT, preferred_element_type=F32) * (hd ** -0.5)
        e = jnp.exp(s - jnp.max(s, axis=-1, keepdims=True))
        p = (e / jnp.sum(e, axis=-1, keepdims=True)).astype(BF16)
        o_ref[:, sl] = jnp.dot(p, v_ref[:, sl].astype(BF16), preferred_element_type=F32)


def _mem_attn(z, g, k_arr, k_spec, v_arr, v_spec, *, n_batch, t_len, tt, row_off, out_rows, alias=None):
    mem_w = g.shape[1] * MEM_HEADS
    col_blk = z.shape[1] // mem_w - 1
    nt = t_len // tt
    body, extra, extra_specs, kw = _with_alias(_mem_attn_body, 4, alias)
    return pl.pallas_call(
        body,
        out_shape=jax.ShapeDtypeStruct((out_rows, mem_w), F32),
        grid=(n_batch, nt),
        in_specs=[
            pl.BlockSpec((tt, mem_w), lambda b, t: ((row_off + b * t_len) // tt + t, col_blk)),
            pl.BlockSpec((1, g.shape[1]), lambda b, t: (0, 0)),
            k_spec,
            v_spec,
        ] + extra_specs,
        out_specs=pl.BlockSpec((tt, mem_w), lambda b, t: ((row_off + b * t_len) // tt + t, 0)),
        compiler_params=_params("parallel", "arbitrary"),
        name="mem_attn",
        **kw,
    )(z, g, k_arr, v_arr, *extra)


def _out_proj_body(tok_ref, mem_ref, w_ref, x_ref, o_ref, cat_ref):
    tok_w = tok_ref.shape[1]

    @pl.when(pl.program_id(1) == 0)
    def _():
        cat_ref[:, :tok_w] = tok_ref[...].astype(cat_ref.dtype)
        cat_ref[:, tok_w:] = mem_ref[...].astype(cat_ref.dtype)

    o_ref[...] = x_ref[...] + _matmul(cat_ref[...], w_ref[...])


def _out_proj(tok, mem, w, x, *, layer, tm, tn, in_blk_off=0, out_rows=None, alias=None):
    m, d = x.shape
    tok_w, mem_w = tok.shape[1], mem.shape[1]
    out_rows = m if out_rows is None else out_rows
    body, extra, extra_specs, kw = _with_alias(_out_proj_body, 4, alias)
    return pl.pallas_call(
        body,
        out_shape=jax.ShapeDtypeStruct((out_rows, d), F32),
        grid=(pl.cdiv(m, tm), d // tn),
        in_specs=[
            pl.BlockSpec((tm, tok_w), lambda i, j: (i + in_blk_off, 0)),
            pl.BlockSpec((tm, mem_w), lambda i, j: (i + in_blk_off, 0)),
            pl.BlockSpec((None, tok_w + mem_w, tn), lambda i, j: (layer, 0, j)),
            pl.BlockSpec((tm, tn), lambda i, j: (i, j)),
        ] + extra_specs,
        out_specs=pl.BlockSpec((tm, tn), lambda i, j: (i + in_blk_off, j)),
        scratch_shapes=[pltpu.VMEM((tm, tok_w + mem_w), w.dtype)],
        compiler_params=_params("parallel", "arbitrary"),
        name="out_proj",
        **kw,
    )(tok, mem, w, x, *extra)


def _route_body(x_ref, g_ref, wh_ref, wl_ref, b_ref, info_ref, *, n_groups, per_group):
    h = _rms(x_ref[...], g_ref[...])
    hh, hl = _split_bf16(h)
    d = lambda a, b: jnp.dot(a, b, preferred_element_type=F32)
    logits = d(hh, wh_ref[...]) + (d(hh, wl_ref[...]) + d(hl, wh_ref[...])) + b_ref[...]
    lane = lax.broadcasted_iota(jnp.int32, logits.shape, 1)
    big = jnp.int32(ROUTE_LANES)

    def first_max(mask):
        v = jnp.max(jnp.where(mask, logits, NEG), axis=-1, keepdims=True)
        i = jnp.min(jnp.where(mask & (logits == v), lane, big), axis=-1, keepdims=True)
        return v, i

    g_mask = lane < n_groups
    g_max, g_sel = first_max(g_mask)
    p_group = 1.0 / jnp.sum(jnp.where(g_mask, jnp.exp(logits - g_max), 0.0), axis=-1, keepdims=True)
    lo = n_groups + g_sel * per_group
    e_mask = (lane >= lo) & (lane < lo + per_group)
    v1, i1 = first_max(e_mask)
    v2, i2 = first_max(e_mask & (lane != i1))
    e2 = jnp.exp(v2 - v1)
    w1 = p_group / (1.0 + e2)
    w2 = p_group * e2 / (1.0 + e2)
    id1 = (i1 - n_groups).astype(F32)
    id2 = (i2 - n_groups).astype(F32)
    info_ref[...] = jnp.where(lane == 0, id1, jnp.where(lane == 1, id2,
                              jnp.where(lane == 2, w1, jnp.where(lane == 3, w2, 0.0))))


def _route(x, g, w_hi, w_lo, bias, *, layer, tm, n_groups, per_group):
    m, d = x.shape
    body = functools.partial(_route_body, n_groups=n_groups, per_group=per_group)
    return pl.pallas_call(
        body,
        out_shape=jax.ShapeDtypeStruct((m, ROUTE_LANES), F32),
        grid=(pl.cdiv(m, tm),),
        in_specs=[
            pl.BlockSpec((tm, d), lambda i: (i, 0)),
            pl.BlockSpec((1, d), lambda i: (0, 0)),
            pl.BlockSpec((None, d, ROUTE_LANES), lambda i: (layer, 0, 0)),
            pl.BlockSpec((None, d, ROUTE_LANES), lambda i: (layer, 0, 0)),
            pl.BlockSpec((None, 1, ROUTE_LANES), lambda i: (layer, 0, 0)),
        ],
        out_specs=pl.BlockSpec((tm, ROUTE_LANES), lambda i: (i, 0)),
        compiler_params=_params("parallel"),
        name="moe_route",
    )(x, g, w_hi, w_lo, bias)


def _moe_plan(info, n_experts, tm, sub, n_pad_pairs):
    m = info.shape[0]
    ids = info[:, :EXPERT_TOPK].astype(jnp.int32).reshape(-1)
    n_pairs = m * EXPERT_TOPK
    n_tiles = n_pairs // tm + n_experts
    n_slots = n_tiles * tm
    onehot = (ids[:, None] == jnp.arange(n_experts)[None, :]).astype(jnp.int32)
    csum = jnp.cumsum(onehot, axis=0)
    rank = jnp.sum(csum * onehot, axis=1) - 1
    counts = csum[-1]
    tiles_e = (counts + tm - 1) // tm
    tile_end = jnp.cumsum(tiles_e)
    tile_start = tile_end - tiles_e
    slot_of_pair = jnp.sum(onehot * tile_start[None, :], axis=1) * tm + rank
    total = tile_end[-1]
    t_idx = jnp.arange(n_tiles, dtype=jnp.int32)
    t_eff = jnp.minimum(t_idx, total - 1).astype(jnp.int32)
    tile_expert = jnp.sum((tile_end[None, :] <= t_eff[:, None]).astype(jnp.int32), axis=1)
    te_hot = (tile_expert[:, None] == jnp.arange(n_experts)[None, :]).astype(jnp.int32)
    rows_left = jnp.sum(te_hot * (counts[None, :] - (t_eff[:, None] - tile_start[None, :]) * tm), axis=1)
    tile_subs = jnp.where(t_idx < total, (jnp.clip(rows_left, 0, tm) + sub - 1) // sub, 0).astype(jnp.int32)
    pair = jnp.arange(n_pairs, dtype=jnp.int32)
    slot_src = jnp.zeros((n_slots,), jnp.int32).at[slot_of_pair].set(pair // EXPERT_TOPK, unique_indices=True)
    pair_slot = jnp.concatenate([slot_of_pair.astype(jnp.int32), jnp.zeros((n_pad_pairs - n_pairs,), jnp.int32)])
    return slot_src, pair_slot, tile_expert.astype(jnp.int32), t_eff, tile_subs


def _gate_up_body(te_ref, tb_ref, ts_ref, src_ref, x_hbm, g_ref, wg_ref, wu_ref, act_ref, xbuf, xn_ref, sem,
                  *, tm, sub, n_tiles):
    t, c = pl.program_id(0), pl.program_id(1)
    nsub = tm // sub
    n_here = ts_ref[t]

    def start_sub(tile, sb):
        def issue(i, carry):
            r = sb * sub + i
            tok = src_ref[tile * tm + r]
            pltpu.make_async_copy(x_hbm.at[pl.ds(tok, 1)], xbuf.at[pl.ds(r, 1)], sem.at[0]).start()
            return carry
        lax.fori_loop(0, sub, issue, 0, unroll=DMA_UNROLL)

    @pl.when((t == 0) & (c == 0))
    def _():
        for sb in range(nsub):
            pl.when(sb < n_here)(functools.partial(start_sub, 0, sb))

    @pl.when((c == 0) & (n_here > 0))
    def _():
        for sb in range(nsub):
            @pl.when(sb < n_here)
            def _():
                pltpu.make_async_copy(x_hbm.at[pl.ds(0, sub)], xbuf.at[pl.ds(0, sub)], sem.at[0]).wait()
        for sb in range(nsub):
            @pl.when(sb < n_here)
            def _():
                rows = slice(sb * sub, (sb + 1) * sub)
                xn_ref[rows, :] = _rms(xbuf[rows, :], g_ref[...]).astype(BF16)

    nxt = jnp.minimum(t + 1, n_tiles - 1)

    @pl.when((t + 1 < n_tiles) & (c < ts_ref[nxt]))
    def _():
        start_sub(t + 1, c)

    for v in range(1, nsub + 1):
        @pl.when(n_here == v)
        def _():
            x = xn_ref[:v * sub, :]
            gate = jnp.dot(x, wg_ref[...].astype(BF16), preferred_element_type=F32)
            up = jnp.dot(x, wu_ref[...].astype(BF16), preferred_element_type=F32)
            act_ref[:v * sub, :] = (gate * jax.nn.sigmoid(gate) * up).astype(BF16)


def _gate_up(x, g, w_gate_up, slot_src, tile_expert, tile_block, tile_subs, *, layer, tm, sub):
    d = x.shape[1]
    f = w_gate_up.shape[3] // 2
    nc = tm // sub
    fc = f // nc
    n_tiles = tile_expert.shape[0]
    cc = lambda t, c, ts: jnp.where(ts[t] > 0, c, nc - 1)
    body = functools.partial(_gate_up_body, tm=tm, sub=sub, n_tiles=n_tiles)
    grid_spec = pltpu.PrefetchScalarGridSpec(
        num_scalar_prefetch=4,
        grid=(n_tiles, nc),
        in_specs=[
            pl.BlockSpec(memory_space=pl.ANY),
            pl.BlockSpec((1, d), lambda t, c, te, tb, ts, src: (0, 0)),
            pl.BlockSpec((None, None, d, fc), lambda t, c, te, tb, ts, src: (layer, te[t], 0, cc(t, c, ts))),
            pl.BlockSpec((None, None, d, fc), lambda t, c, te, tb, ts, src: (layer, te[t], 0, nc + cc(t, c, ts))),
        ],
        out_specs=pl.BlockSpec((tm, fc), lambda t, c, te, tb, ts, src: (tb[t], cc(t, c, ts))),
        scratch_shapes=[pltpu.VMEM((tm, d), F32), pltpu.VMEM((tm, d), BF16), pltpu.SemaphoreType.DMA((1,))],
    )
    return pl.pallas_call(
        body,
        out_shape=jax.ShapeDtypeStruct((n_tiles * tm, f), BF16),
        grid_spec=grid_spec,
        compiler_params=_params("arbitrary", "arbitrary", gather=True),
        name="moe_gate_up",
    )(tile_expert, tile_block, tile_subs, slot_src, x, g, w_gate_up, w_gate_up)


def _down_body(te_ref, tb_ref, ts_ref, act_ref, wd_ref, o_ref, *, sub):
    n_here = ts_ref[pl.program_id(0)]
    for v in range(1, act_ref.shape[0] // sub + 1):
        @pl.when(n_here == v)
        def _():
            o_ref[:v * sub, :] = jnp.dot(act_ref[:v * sub, :], wd_ref[...].astype(BF16), preferred_element_type=F32)


def _down(act, w_down, tile_expert, tile_block, tile_subs, *, layer, tm, tn, sub):
    f, d = w_down.shape[2:]
    nj = d // tn
    n_tiles = tile_expert.shape[0]
    jc = lambda t, j, ts: jnp.where(ts[t] > 0, j, nj - 1)
    grid_spec = pltpu.PrefetchScalarGridSpec(
        num_scalar_prefetch=3,
        grid=(n_tiles, nj),
        in_specs=[
            pl.BlockSpec((tm, f), lambda t, j, te, tb, ts: (tb[t], 0)),
            pl.BlockSpec((None, None, f, tn), lambda t, j, te, tb, ts: (layer, te[t], 0, jc(t, j, ts))),
        ],
        out_specs=pl.BlockSpec((tm, tn), lambda t, j, te, tb, ts: (tb[t], jc(t, j, ts))),
    )
    return pl.pallas_call(
        functools.partial(_down_body, sub=sub),
        out_shape=jax.ShapeDtypeStruct((n_tiles * tm, d), F32),
        grid_spec=grid_spec,
        compiler_params=_params("arbitrary", "arbitrary"),
        name="moe_down",
    )(tile_expert, tile_block, tile_subs, act, w_down)


def _combine_body(ps_ref, x_ref, info_ref, ys_hbm, o_ref, gbuf, sem, *, tm, row_off, n_steps):
    i = pl.program_id(0)
    slot = i % 2

    def start_tile(step, buf):
        def issue(r, carry):
            for k in range(EXPERT_TOPK):
                s = ps_ref[(row_off + step * tm + r) * EXPERT_TOPK + k]
                pltpu.make_async_copy(ys_hbm.at[pl.ds(s, 1)], gbuf.at[buf, k, pl.ds(r, 1)], sem.at[buf]).start()
            return carry
        lax.fori_loop(0, tm, issue, 0, unroll=DMA_UNROLL // EXPERT_TOPK)

    @pl.when(i == 0)
    def _():
        start_tile(0, 0)

    @pl.when(i + 1 < n_steps)
    def _():
        start_tile(i + 1, 1 - slot)

    for k in range(EXPERT_TOPK):
        pltpu.make_async_copy(ys_hbm.at[pl.ds(0, tm)], gbuf.at[slot, k], sem.at[slot]).wait()
    w = info_ref[...]
    y = x_ref[...]
    for k in range(EXPERT_TOPK):
        y = y + w[:, EXPERT_TOPK + k:EXPERT_TOPK + k + 1] * gbuf[slot, k]
    o_ref[...] = y


def _combine(x, info, ys, pair_slot, *, tm, row_blk_off=0, n_rows=None):
    d = x.shape[1]
    n_rows = x.shape[0] if n_rows is None else n_rows
    n_steps = pl.cdiv(n_rows, tm)
    body = functools.partial(_combine_body, tm=tm, row_off=row_blk_off * tm, n_steps=n_steps)
    grid_spec = pltpu.PrefetchScalarGridSpec(
        num_scalar_prefetch=1,
        grid=(n_steps,),
        in_specs=[
            pl.BlockSpec((tm, d), lambda i, ps: (i + row_blk_off, 0)),
            pl.BlockSpec((tm, ROUTE_LANES), lambda i, ps: (i + row_blk_off, 0)),
            pl.BlockSpec(memory_space=pl.ANY),
        ],
        out_specs=pl.BlockSpec((tm, d), lambda i, ps: (i, 0)),
        scratch_shapes=[pltpu.VMEM((2, EXPERT_TOPK, tm, d), F32), pltpu.SemaphoreType.DMA((2,))],
    )
    return pl.pallas_call(
        body,
        out_shape=jax.ShapeDtypeStruct((n_rows, d), F32),
        grid_spec=grid_spec,
        compiler_params=_params("arbitrary", gather=True),
        name="moe_combine",
    )(pair_slot, x, info, ys)


def _moe_experts(x, g, w_rt_hi, w_rt_lo, b_rt, w_gate_up, w_down, *, layer, n_groups, per_group, tm_route, tm, sub,
                 tn, tm_c):
    m = x.shape[0]
    info = _route(x, g, w_rt_hi, w_rt_lo, b_rt, layer=layer, tm=tm_route, n_groups=n_groups, per_group=per_group)
    n_pad_pairs = pl.cdiv(m, tm_c) * tm_c * EXPERT_TOPK
    slot_src, pair_slot, te, tb, ts = _moe_plan(info, n_groups * per_group, tm, sub, n_pad_pairs)
    act = _gate_up(x, g, w_gate_up, slot_src, te, tb, ts, layer=layer, tm=tm, sub=sub)
    ys = _down(act, w_down, te, tb, ts, layer=layer, tm=tm, tn=tn, sub=sub)
    return info, ys, pair_slot


def _moba_prompt_body(slopes_ref, q_ref, gq_ref, k_ref, v_ref, o_ref, *, kv_group, blk):
    g, own = pl.program_id(1), pl.program_id(2)
    hd = k_ref.shape[1]
    nb = k_ref.shape[0] // blk
    cols = kv_group * blk
    q = q_ref[...]
    qn = jnp.concatenate([_rms(q[:, r * hd:(r + 1) * hd], gq_ref[...]) for r in range(kv_group)], axis=0)
    qb = qn.astype(BF16)
    qcol = lax.broadcasted_iota(jnp.int32, (1, cols), 1)
    slope = jnp.zeros((1, cols), F32)
    for r in range(kv_group):
        slope = jnp.where(qcol // blk == r, slopes_ref[g * kv_group + r], slope)
    tq = qcol % blk
    key = lax.broadcasted_iota(jnp.int32, (blk, 1), 0)
    rel = slope * (tq - key).astype(F32)
    scale = hd ** -0.5

    nbp = -(-nb // 8) * 8
    km = jnp.concatenate([jnp.mean(k_ref[n * blk:(n + 1) * blk, :], axis=0, keepdims=True) for n in range(nb)]
                         + [jnp.zeros((1, hd), F32)] * (nbp - nb), axis=0)
    blk_id = lax.broadcasted_iota(jnp.int32, (nbp, cols), 0)
    gate = jnp.where(blk_id < own, _dot3(km, qn, _NT), NEG)
    cnt = jnp.zeros(gate.shape, jnp.int32)
    for m_ in range(nb):
        gm = gate[m_:m_ + 1, :]
        ahead = jnp.where(gm > gate, 1, jnp.where(gm == gate, jnp.where(m_ < blk_id, 1, 0), 0))
        cnt = cnt + ahead
    skip = jnp.where(cnt < MOBA_TOPK, jnp.where(blk_id < own, 0.0, -NEG), -NEG)
    pen = skip + slope * ((own - blk_id) * blk).astype(F32)

    def block_scores(n):
        start = pl.multiple_of(n * blk, blk)
        kb = k_ref[pl.ds(start, blk), :].astype(BF16)
        vt = v_ref[pl.ds(start, blk), :].T.astype(BF16)
        s = lax.dot_general(kb, qb, _NT, preferred_element_type=F32) * scale
        return s - rel, vt

    s, vt = block_scores(own)
    s = jnp.where(key <= tq, s, NEG)
    m0 = jnp.max(s, axis=0, keepdims=True)
    p = jnp.exp(s - m0)
    l0 = jnp.sum(p, axis=0, keepdims=True)
    a0 = jnp.dot(vt, p.astype(BF16), preferred_element_type=F32)

    def step(n, carry):
        m_run, l_run, acc = carry
        s, vt = block_scores(n)
        s = s - jnp.sum(jnp.where(blk_id == n, pen, 0.0), axis=0, keepdims=True)
        m_new = jnp.maximum(m_run, jnp.max(s, axis=0, keepdims=True))
        alpha = jnp.exp(m_run - m_new)
        p = jnp.exp(s - m_new)
        l_new = alpha * l_run + jnp.sum(p, axis=0, keepdims=True)
        acc = alpha * acc + jnp.dot(vt, p.astype(BF16), preferred_element_type=F32)
        return m_new, l_new, acc

    _, l_fin, acc = lax.fori_loop(0, own, step, (m0, l0, a0))
    out = (acc / l_fin).T
    for r in range(kv_group):
        o_ref[:, r * hd:(r + 1) * hd] = out[r * blk:(r + 1) * blk]


def _moba_prompt(z, gq, kv, slopes, *, n_batch, t_len, n_kv, kv_group, out_rows):
    hd = gq.shape[1]
    blk = MOBA_BLOCK
    gw = kv_group * hd
    nt = t_len // blk
    body = functools.partial(_moba_prompt_body, kv_group=kv_group, blk=blk)
    grid_spec = pltpu.PrefetchScalarGridSpec(
        num_scalar_prefetch=1,
        grid=(n_batch, n_kv, nt),
        in_specs=[
            pl.BlockSpec((blk, gw), lambda b, g, t, s: (b * nt + t, g)),
            pl.BlockSpec((1, hd), lambda b, g, t, s: (0, 0)),
            pl.BlockSpec((t_len, hd), lambda b, g, t, s: (b, g)),
            pl.BlockSpec((t_len, hd), lambda b, g, t, s: (b, n_kv + g)),
        ],
        out_specs=pl.BlockSpec((blk, gw), lambda b, g, t, s: (b * nt + t, g)),
    )
    return pl.pallas_call(
        body,
        out_shape=jax.ShapeDtypeStruct((out_rows, n_kv * gw), F32),
        grid_spec=grid_spec,
        compiler_params=_params("parallel", "parallel", "arbitrary"),
        name="moba_prompt",
    )(slopes, z, gq, kv, kv)


def _sample_queries(q, gq, n_heads, hd):
    return jnp.concatenate([_rms(q[:, h * hd:(h + 1) * hd], gq) for h in range(n_heads)], axis=0)


def _row_slopes(slopes_ref, n_heads, t_len):
    row = lax.broadcasted_iota(jnp.int32, (n_heads * t_len, 1), 0)
    slope = jnp.zeros((n_heads * t_len, 1), F32)
    for h in range(n_heads):
        slope = jnp.where(row // t_len == h, slopes_ref[h], slope)
    return slope


def _moba_partial_body(pt_ref, slopes_ref, q_ref, gq_ref, k0_ref, k1_ref, v0_ref, v1_ref, po_ref, ps_ref,
                       qn_ref, bias_ref, *, kv_group, past_len):
    n = pl.program_id(1)
    page, n_kv, hd = k0_ref.shape
    t_len = q_ref.shape[0]
    n_heads = n_kv * kv_group
    rows = n_heads * t_len
    blk = 2 * page
    cols = blk * n_kv
    row = lax.broadcasted_iota(jnp.int32, (rows, 1), 0)

    @pl.when(n == 0)
    def _():
        qn_ref[...] = _sample_queries(q_ref[...], gq_ref[...], n_heads, hd)
        col = lax.broadcasted_iota(jnp.int32, (1, cols), 1)
        same_head = row // (kv_group * t_len) == col % n_kv
        slope = _row_slopes(slopes_ref, n_heads, t_len)
        bias_ref[...] = jnp.where(same_head, slope * (past_len + row % t_len - col // n_kv).astype(F32), -NEG)

    qn = qn_ref[...]
    slope = _row_slopes(slopes_ref, n_heads, t_len)
    kb = jnp.concatenate([k0_ref[...].reshape(page * n_kv, hd), k1_ref[...].reshape(page * n_kv, hd)], axis=0)
    vb = jnp.concatenate([v0_ref[...].reshape(page * n_kv, hd), v1_ref[...].reshape(page * n_kv, hd)], axis=0)
    s = lax.dot_general(qn.astype(BF16), kb.astype(BF16), _NT, preferred_element_type=F32) * (hd ** -0.5)
    s = s - (bias_ref[...] - slope * (n * blk).astype(F32))
    mx = jnp.max(s, axis=-1, keepdims=True)
    p = jnp.exp(s - mx)
    den = jnp.sum(p, axis=-1, keepdims=True)
    po_ref[...] = jnp.dot(p.astype(BF16), vb.astype(BF16), preferred_element_type=F32)
    km = (jnp.sum(k0_ref[...], axis=0) + jnp.sum(k1_ref[...], axis=0)) / blk
    km_rows = jnp.concatenate([jnp.broadcast_to(km[g:g + 1], (kv_group * t_len, hd)) for g in range(n_kv)], axis=0)
    gate = jnp.sum(qn * km_rows, axis=-1, keepdims=True)
    lane = lax.broadcasted_iota(jnp.int32, (rows, ROUTE_LANES), 1)
    ps_ref[...] = jnp.where(lane == 0, mx, jnp.where(lane == 1, den, jnp.where(lane == 2, gate, 0.0)))


def _moba_partials(z, gq, cache_k, cache_v, page_table, slopes, *, n_batch, t_len, row_off, kv_group, past_len):
    page, n_kv, hd = cache_k.shape[1:]
    pages_per_blk = MOBA_BLOCK // page
    assert pages_per_blk == 2
    n_pages = page_table.shape[1]
    nblk = n_pages // pages_per_blk
    n_heads = n_kv * kv_group
    rows = n_heads * t_len
    body = functools.partial(_moba_partial_body, kv_group=kv_group, past_len=past_len)
    page_spec = lambda which: pl.BlockSpec(
        (None, page, n_kv, hd), lambda b, n, pt, s: (pt[b * n_pages + n * pages_per_blk + which], 0, 0, 0))
    grid_spec = pltpu.PrefetchScalarGridSpec(
        num_scalar_prefetch=2,
        grid=(n_batch, nblk),
        in_specs=[
            pl.BlockSpec((t_len, n_heads * hd), lambda b, n, pt, s: (row_off // t_len + b, 0)),
            pl.BlockSpec((1, hd), lambda b, n, pt, s: (0, 0)),
            page_spec(0), page_spec(1), page_spec(0), page_spec(1),
        ],
        out_specs=(
            pl.BlockSpec((None, None, rows, hd), lambda b, n, pt, s: (b, n, 0, 0)),
            pl.BlockSpec((None, None, rows, ROUTE_LANES), lambda b, n, pt, s: (b, n, 0, 0)),
        ),
        scratch_shapes=[pltpu.VMEM((rows, hd), F32), pltpu.VMEM((rows, MOBA_BLOCK * n_kv), F32)],
    )
    return pl.pallas_call(
        body,
        out_shape=(jax.ShapeDtypeStruct((n_batch, nblk, rows, hd), F32),
                   jax.ShapeDtypeStruct((n_batch, nblk, rows, ROUTE_LANES), F32)),
        grid_spec=grid_spec,
        compiler_params=_params("parallel", "arbitrary"),
        name="moba_partials",
    )(page_table.reshape(-1), slopes, z, gq, cache_k, cache_k, cache_v, cache_v)


def _moba_merge_body(slopes_ref, q_ref, gq_ref, kn_ref, vn_ref, po_ref, ps_ref, o_ref, *, n_kv, kv_group):
    hd = gq_ref.shape[1]
    t_len = q_ref.shape[0]
    n_heads = n_kv * kv_group
    rows = n_heads * t_len
    grows = kv_group * t_len
    nblk = ps_ref.shape[0]
    lane = lax.broadcasted_iota(jnp.int32, (rows, ROUTE_LANES), 1)
    m_blk = jnp.full((rows, ROUTE_LANES), NEG, F32)
    l_blk = jnp.zeros((rows, ROUTE_LANES), F32)
    gates = jnp.full((rows, ROUTE_LANES), NEG, F32)
    for n in range(nblk):
        st = ps_ref[n]
        m_blk = jnp.where(lane == n, st[:, 0:1], m_blk)
        l_blk = jnp.where(lane == n, st[:, 1:2], l_blk)
        gates = jnp.where(lane == n, st[:, 2:3], gates)
    w_blk = jnp.zeros((rows, ROUTE_LANES), F32)
    for _ in range(min(MOBA_TOPK, nblk)):
        best = jnp.max(gates, axis=-1, keepdims=True)
        first = jnp.min(jnp.where(gates == best, lane, ROUTE_LANES), axis=-1, keepdims=True)
        w_blk = jnp.where(lane == first, 1.0, w_blk)
        gates = jnp.where(lane == first, NEG, gates)
    m_blk = jnp.where(w_blk > 0.0, m_blk, NEG)
    m_past = jnp.max(m_blk, axis=-1, keepdims=True)

    qn = _sample_queries(q_ref[...], gq_ref[...], n_heads, hd)
    slope = _row_slopes(slopes_ref, n_heads, t_len)
    row = lax.broadcasted_iota(jnp.int32, (rows, 1), 0)
    tq = row % t_len
    tk = lax.broadcasted_iota(jnp.int32, (1, t_len), 1)
    s_own = jnp.concatenate([
        lax.dot_general(qn[g * grows:(g + 1) * grows].astype(BF16), kn_ref[:, g * hd:(g + 1) * hd].astype(BF16), _NT,
                        preferred_element_type=F32) for g in range(n_kv)], axis=0) * (hd ** -0.5)
    s_own = jnp.where(tk <= tq, s_own - slope * (tq - tk).astype(F32), NEG)
    m_all = jnp.maximum(m_past, jnp.max(s_own, axis=-1, keepdims=True))
    w_blk = jnp.where(w_blk > 0.0, jnp.exp(m_blk - m_all), 0.0)
    p_own = jnp.exp(s_own - m_all)
    den = jnp.sum(p_own, axis=-1, keepdims=True) + jnp.sum(w_blk * l_blk, axis=-1, keepdims=True)
    num = jnp.zeros((rows, hd), F32)
    for n in range(nblk):
        num = num + w_blk[:, n:n + 1] * po_ref[n]
    num = num + jnp.concatenate([
        jnp.dot(p_own[g * grows:(g + 1) * grows].astype(BF16), vn_ref[:, g * hd:(g + 1) * hd].astype(BF16),
                preferred_element_type=F32) for g in range(n_kv)], axis=0)
    out = num / den
    for h in range(n_heads):
        o_ref[:, h * hd:(h + 1) * hd] = out[h * t_len:(h + 1) * t_len]


def _moba_merge(z, gq, kv, part_o, part_s, slopes, *, n_batch, t_len, row_off, n_kv, kv_group, alias):
    hd = gq.shape[1]
    kw = n_kv * hd
    nblk, rows = part_o.shape[1:3]
    body = functools.partial(_moba_merge_body, n_kv=n_kv, kv_group=kv_group)
    wrapped = lambda *refs: body(*refs[:7], *refs[8:])
    new_rows = lambda b, s: (row_off // t_len + b, 0)
    grid_spec = pltpu.PrefetchScalarGridSpec(
        num_scalar_prefetch=1,
        grid=(n_batch,),
        in_specs=[
            pl.BlockSpec((t_len, kv_group * kw), new_rows),
            pl.BlockSpec((1, hd), lambda b, s: (0, 0)),
            pl.BlockSpec((t_len, kw), new_rows),
            pl.BlockSpec((t_len, kw), lambda b, s: (row_off // t_len + b, 1)),
            pl.BlockSpec((None, nblk, rows, hd), lambda b, s: (b, 0, 0, 0)),
            pl.BlockSpec((None, nblk, rows, ROUTE_LANES), lambda b, s: (b, 0, 0, 0)),
            pl.BlockSpec(memory_space=pl.ANY),
        ],
        out_specs=pl.BlockSpec((t_len, kv_group * kw), new_rows),
    )
    return pl.pallas_call(
        wrapped,
        out_shape=jax.ShapeDtypeStruct(alias.shape, F32),
        grid_spec=grid_spec,
        input_output_aliases={7: 0},
        compiler_params=_params("parallel"),
        name="moba_merge",
    )(slopes, z, gq, kv, kv, part_o, part_s, alias)


def _tile(m, pref):
    return pref if m >= pref else m


def kernel(x_prompt, x_sample, mem_prompt, cache_k, cache_v, cache_mem_k, cache_mem_v, state_pool, page_table, g_mix_norm, w_in, w_out, w_pool, pool_scale, g_q, g_kv_norm, w_kv, g_k, g_mem_norm, w_mem_kv, g_mem_q, g_mem_k, g_ffn_norm, w_group, b_group, w_router, b_router, w_gate_up, w_down):
    bp, tp, d = x_prompt.shape
    bs, ts, _ = x_sample.shape
    depth = w_in.shape[0]
    n_a = w_pool.shape[0]
    mem_tokens = mem_prompt.shape[1]
    mem_w = d // 4
    mem_hd = mem_w // MEM_HEADS
    tok_w = d - mem_w
    page, n_kv, hd = cache_k.shape[1:]
    kv_w = n_kv * hd
    n_heads = tok_w // hd
    kv_group = n_heads // n_kv
    past_len = page_table.shape[1] * page
    n_groups = w_group.shape[2]
    n_experts = w_router.shape[2]
    per_group = n_experts // n_groups
    d_expert = w_down.shape[2]
    mp, ms = bp * tp, bs * ts
    m_all = mp + ms

    tm_p = _tile(mp, 512)
    tn = _tile(d, 1024)
    tn_s = _tile(d, 512)
    tt_p = _tile(tp, MOBA_BLOCK)
    tn_e = _tile(d, 2048)
    tm_e = 1024 if m_all >= 4096 else 128
    sub_e = tm_e // 4
    tm_c = _tile(mp, 256)

    row = lambda v: v.reshape(1, -1).astype(F32)
    w_in_b, w_out_b, w_pool_b = w_in.astype(BF16), w_out.astype(BF16), w_pool.astype(BF16)
    w_kv_b, w_mem_kv_b = w_kv.astype(BF16)[None], w_mem_kv.astype(BF16)
    w_rt = jnp.concatenate([w_group, w_router, jnp.zeros((depth, d, ROUTE_LANES - n_groups - n_experts), F32)], axis=-1)
    w_rt_hi = w_rt.astype(BF16)
    w_rt_lo = (w_rt - w_rt_hi.astype(F32)).astype(BF16)
    b_rt = jnp.concatenate([b_group, b_router, jnp.zeros((depth, ROUTE_LANES - n_groups - n_experts), F32)],
                           axis=-1)[:, None, :]
    slopes = jnp.exp2(-8.0 * jnp.arange(1, n_heads + 1, dtype=F32) / n_heads)

    xp = x_prompt.reshape(mp, d)
    xs_ = x_sample.reshape(ms, d)
    sample_blk = mp // ms

    memx = mem_prompt.reshape(bp * mem_tokens, d)
    mem_kv_p = [
        _norm_matmul(memx, row(g_mem_norm[l]), w_mem_kv_b, layer=l, n_out=2 * mem_w, tm=_tile(bp * mem_tokens, 512),
                     tn=mem_w, head_groups=(mem_hd, 0),
                     gh=jnp.concatenate([jnp.tile(row(g_mem_k[l]), (1, MEM_HEADS)), jnp.ones((1, mem_w), F32)], axis=1))
        for l in range(depth)]
    mem_kv_p3 = [a.reshape(bp, mem_tokens, 2 * mem_w) for a in mem_kv_p]
    cmk = cache_mem_k.reshape(depth, bs, mem_tokens, mem_w)
    cmv = cache_mem_v.reshape(depth, bs, mem_tokens, mem_w)

    def mem_attend(z, l):
        y = _mem_attn(z, row(g_mem_q[l]),
                      mem_kv_p3[l], pl.BlockSpec((None, mem_tokens, mem_w), lambda b, t: (b, 0, 0)),
                      mem_kv_p3[l], pl.BlockSpec((None, mem_tokens, mem_w), lambda b, t: (b, 0, 1)),
                      n_batch=bp, t_len=tp, tt=tt_p, row_off=0, out_rows=m_all)
        return _mem_attn(z, row(g_mem_q[l]),
                         cmk, pl.BlockSpec((None, None, mem_tokens, mem_w), lambda b, t: (l, b, 0, 0)),
                         cmv, pl.BlockSpec((None, None, mem_tokens, mem_w), lambda b, t: (l, b, 0, 0)),
                         n_batch=bs, t_len=ts, tt=ts, row_off=mp, out_rows=m_all, alias=y)

    def moe_experts(x, l):
        return _moe_experts(x, row(g_ffn_norm[l]), w_rt_hi, w_rt_lo, b_rt, w_gate_up, w_down, layer=l,
                            n_groups=n_groups, per_group=per_group, tm_route=tm_p, tm=tm_e, sub=sub_e, tn=tn_e,
                            tm_c=tm_c)

    x = None
    z_pool = []
    kv = None
    y_prompt = y_sample = None
    for l in range(depth):
        g_mix = row(g_mix_norm[l])
        if l == 0:
            z = _norm_matmul(xp, g_mix, w_in_b, layer=l, n_out=d, tm=tm_p, tn=tn, out_rows=m_all)
            z = _norm_matmul(xs_, g_mix, w_in, layer=l, n_out=d, tm=ms, tn=tn_s, out_rows=m_all,
                             row_blk_off=sample_blk, alias=z)
        else:
            z = _norm_matmul(x, g_mix, w_in_b, layer=l, n_out=d, tm=tm_p, tn=tn)
        mem_y = mem_attend(z, l)
        if l < n_a:
            z_pool.append(z)
            scale = row(pool_scale[l])
            zero_buf = jnp.zeros((bp, POOL_HALO, tok_w), F32)
            samp_buf = jnp.concatenate([jnp.zeros((bs, 1, tok_w), F32), state_pool[l]], axis=1)
            tok_y = _pool_mix(z, zero_buf, w_pool_b[l], scale, n_batch=bp, t_len=tp, tt=tt_p, row_off=0, pos0=0,
                              out_rows=m_all)
            tok_y = _pool_mix(z, samp_buf, w_pool[l], scale, n_batch=bs, t_len=ts, tt=ts, row_off=mp,
                              pos0=past_len, out_rows=m_all, alias=tok_y)
        else:
            if kv is None:
                kv = _norm_matmul(x, row(g_kv_norm), w_kv_b, layer=0, n_out=2 * kv_w, tm=tm_p, tn=kv_w,
                                  head_groups=(hd, 0),
                                  gh=jnp.concatenate([jnp.tile(row(g_k), (1, n_kv)), jnp.ones((1, kv_w), F32)], axis=1))
            gq = row(g_q[l - n_a])
            tok_y = _moba_prompt(z, gq, kv, slopes, n_batch=bp, t_len=tp, n_kv=n_kv, kv_group=kv_group,
                                 out_rows=m_all)
            part_o, part_s = _moba_partials(z, gq, cache_k, cache_v, page_table, slopes, n_batch=bs, t_len=ts,
                                            row_off=mp, kv_group=kv_group, past_len=past_len)
            tok_y = _moba_merge(z, gq, kv, part_o, part_s, slopes, n_batch=bs, t_len=ts, row_off=mp,
                                n_kv=n_kv, kv_group=kv_group, alias=tok_y)
        if l == 0:
            x = _out_proj(tok_y, mem_y, w_out_b, xp, layer=l, tm=tm_p, tn=tn, out_rows=m_all)
            x = _out_proj(tok_y, mem_y, w_out, xs_, layer=l, tm=ms, tn=tn_s, in_blk_off=sample_blk, out_rows=m_all,
                          alias=x)
        else:
            x = _out_proj(tok_y, mem_y, w_out_b, x, layer=l, tm=tm_p, tn=tn)
        info, ys, pair_slot = moe_experts(x, l)
        if l + 1 < depth:
            x = _combine(x, info, ys, pair_slot, tm=tm_c)
        else:
            y_prompt = _combine(x, info, ys, pair_slot, tm=tm_c, n_rows=mp)
            y_sample = _combine(x, info, ys, pair_slot, tm=ms, row_blk_off=sample_blk, n_rows=ms)

    y_prompt = y_prompt.reshape(bp, tp, d)
    y_sample = y_sample.reshape(bs, ts, d)
    k_prompt = kv[:mp, :kv_w].reshape(bp, tp, n_kv, hd)
    v_prompt = kv[:mp, kv_w:].reshape(bp, tp, n_kv, hd)
    k_sample = kv[mp:, :kv_w].reshape(bs, ts, n_kv, hd)
    v_sample = kv[mp:, kv_w:].reshape(bs, ts, n_kv, hd)
    mem_k_prompt = jnp.stack([a[:, :, :mem_w].reshape(bp, mem_tokens, MEM_HEADS, mem_hd) for a in mem_kv_p3])
    mem_v_prompt = jnp.stack([a[:, :, mem_w:].reshape(bp, mem_tokens, MEM_HEADS, mem_hd) for a in mem_kv_p3])
    pool_prompt = jnp.stack([zz[:mp].reshape(bp, tp, d)[:, tp - POOL_BUF:, :tok_w] for zz in z_pool])
    pool_sample = jnp.stack([
        jnp.concatenate([state_pool[i], zz[mp:].reshape(bs, ts, d)[:, :, :tok_w]], axis=1)[:, -POOL_BUF:]
        for i, zz in enumerate(z_pool)])
    return (y_prompt, y_sample, k_prompt, v_prompt, k_sample, v_sample,
            mem_k_prompt, mem_v_prompt, pool_prompt, pool_sample)
```

```python
import functools

import jax
import jax.numpy as jnp
from jax import lax
from jax.experimental import pallas as pl
from jax.experimental.pallas import tpu as pltpu

F32 = jnp.float32
BF16 = jnp.bfloat16

EPS = 1e-6
POOL_WINDOWS = (2, 4, 8, 16)
POOL_BUF = max(POOL_WINDOWS) - 1
POOL_HALO = POOL_BUF + 1
MEM_HEADS = 4
MOBA_BLOCK = 256
MOBA_TOPK = 3
EXPERT_TOPK = 2
ROUTE_LANES = 128
NEG = -1e30
LOG2E = 1.4426950408889634
VMEM_LIMIT = 60 * 1024 * 1024
DMA_UNROLL = 8
BIG_ROW_TILE = 1024
NORM_ROWS = 256

_NT = (((1,), (1,)), ((), ()))
_NN = (((1,), (0,)), ((), ()))


def _params(*sem, gather=False):
    return pltpu.CompilerParams(dimension_semantics=sem, vmem_limit_bytes=VMEM_LIMIT, disable_bounds_checks=gather)


def _split_bf16(a):
    hi = a.astype(BF16)
    lo = (a - hi.astype(F32)).astype(BF16)
    return hi, lo


def _dot3(a, b, dims):
    ah, al = _split_bf16(a)
    bh, bl = _split_bf16(b)
    d = lambda x, y: lax.dot_general(x, y, dims, preferred_element_type=F32)
    return d(ah, bh) + (d(ah, bl) + d(al, bh))


def _rms(x, g):
    return x * lax.rsqrt(jnp.mean(x * x, axis=-1, keepdims=True) + EPS) * g


def _matmul(a, w):
    if w.dtype == F32:
        return _dot3(a.astype(F32), w, _NN)
    return jnp.dot(a.astype(BF16), w, preferred_element_type=F32)


def _with_alias(body, n_in, alias):
    if alias is None:
        return body, [], [], {}
    wrapped = lambda *refs: body(*refs[:n_in], *refs[n_in + 1:])
    return wrapped, [alias], [pl.BlockSpec(memory_space=pl.ANY)], {"input_output_aliases": {n_in: 0}}


def _norm_matmul_body(x_ref, g_ref, w_ref, gh_ref, o_ref, hn_ref, *, head_groups):
    j = pl.program_id(1)

    @pl.when(j == 0)
    def _():
        tm = x_ref.shape[0]
        step = min(tm, NORM_ROWS)
        for r0 in range(0, tm, step):
            hn_ref[r0:r0 + step, :] = _rms(x_ref[r0:r0 + step, :], g_ref[...]).astype(hn_ref.dtype)

    acc = _matmul(hn_ref[...], w_ref[...])
    tn = acc.shape[1]

    def write(gs):
        if gs == 0:
            o_ref[...] = acc
            return
        for c in range(tn // gs):
            sl = slice(c * gs, (c + 1) * gs)
            o_ref[:, sl] = _rms(acc[:, sl], gh_ref[:, sl])

    if len(set(head_groups)) == 1:
        write(head_groups[0])
    else:
        for jj, gs in enumerate(head_groups):
            pl.when(j == jj)(functools.partial(write, gs))


def _row_block_spec(shape, index_map, rows):
    if rows >= BIG_ROW_TILE:
        return pl.BlockSpec(shape, index_map, pipeline_mode=pl.Buffered(1))
    return pl.BlockSpec(shape, index_map)


def _norm_matmul(x, g, w, *, layer, n_out, tm, tn, head_groups=None, gh=None,
                 n_rows=None, in_blk_off=0, out_rows=None, row_blk_off=0, alias=None):
    k = x.shape[1]
    m = x.shape[0] if n_rows is None else n_rows
    nj = n_out // tn
    head_groups = tuple(head_groups) if head_groups else (0,) * nj
    if gh is None:
        gh = jnp.ones((1, n_out), F32)
    out_rows = m if out_rows is None else out_rows
    body = functools.partial(_norm_matmul_body, head_groups=head_groups)
    body, extra, extra_specs, kw = _with_alias(body, 4, alias)
    return pl.pallas_call(
        body,
        out_shape=jax.ShapeDtypeStruct((out_rows, n_out), F32),
        grid=(pl.cdiv(m, tm), nj),
        in_specs=[
            _row_block_spec((tm, k), lambda i, j: (i + in_blk_off, 0), tm),
            pl.BlockSpec((1, k), lambda i, j: (0, 0)),
            pl.BlockSpec((None, k, tn), lambda i, j: (layer, 0, j)),
            pl.BlockSpec((1, tn), lambda i, j: (0, j)),
        ] + extra_specs,
        out_specs=pl.BlockSpec((tm, tn), lambda i, j: (i + row_blk_off, j)),
        scratch_shapes=[pltpu.VMEM((tm, k), w.dtype)],
        compiler_params=_params("parallel", "arbitrary"),
        name="norm_matmul",
        **kw,
    )(x, g, w, gh, *extra)


def _pool_body(cur_ref, halo_ref, buf_ref, w_ref, scale_ref, o_ref, *, tt, pos0):
    t = pl.program_id(1)
    cur = cur_ref[...]
    halo = jnp.where(t == 0, buf_ref[...], halo_ref[...])
    full = jnp.concatenate([halo, cur], axis=0)
    pos = pos0 + t * tt + lax.broadcasted_iota(jnp.int32, (tt, 1), 0)
    gw = cur.shape[1] // len(POOL_WINDOWS)
    for gi, win in enumerate(POOL_WINDOWS):
        sl = slice(gi * gw, (gi + 1) * gw)
        s = full[:, sl]
        sh = 1
        while sh < win:
            s = s + pltpu.roll(s, shift=sh, axis=0)
            sh *= 2
        cnt = jnp.minimum(pos + 1, win).astype(F32)
        d = s[POOL_HALO:] / cnt - cur[:, sl]
        y = _matmul(d, w_ref[gi])
        o_ref[:, sl] = y * scale_ref[:, sl]


def _pool_mix(z, buf16, w_pool, scale, *, n_batch, t_len, tt, row_off, pos0, out_rows, alias=None):
    tok_w = scale.shape[1]
    nt = t_len // tt
    cur_blk = lambda b, t: ((row_off + b * t_len) // tt + t, 0)
    halo_blk = lambda b, t: (jnp.maximum((row_off + b * t_len + t * tt) // POOL_HALO - 1, 0), 0)
    body = functools.partial(_pool_body, tt=tt, pos0=pos0)
    body, extra, extra_specs, kw = _with_alias(body, 5, alias)
    return pl.pallas_call(
        body,
        out_shape=jax.ShapeDtypeStruct((out_rows, tok_w), F32),
        grid=(n_batch, nt),
        in_specs=[
            pl.BlockSpec((tt, tok_w), cur_blk),
            pl.BlockSpec((POOL_HALO, tok_w), halo_blk),
            pl.BlockSpec((None, POOL_HALO, tok_w), lambda b, t: (b, 0, 0)),
            pl.BlockSpec(w_pool.shape, lambda b, t: (0, 0, 0)),
            pl.BlockSpec((1, tok_w), lambda b, t: (0, 0)),
        ] + extra_specs,
        out_specs=pl.BlockSpec((tt, tok_w), cur_blk),
        compiler_params=_params("parallel", "arbitrary"),
        name="pool_mix",
        **kw,
    )(z, z, buf16, w_pool, scale, *extra)


def _mem_attn_body(q_ref, g_ref, k_ref, v_ref, o_ref):
    q = q_ref[...]
    hd = q.shape[1] // MEM_HEADS
    for h in range(MEM_HEADS):
        sl = slice(h * hd, (h + 1) * hd)
        qh = _rms(q[:, sl], g_ref[...]).astype(BF16)
        s = lax.dot_general(qh, k_ref[:, sl].astype(BF16), _NT, preferred_element_type=F32) * (hd ** -0.5)
        e = jnp.exp(s - jnp.max(s, axis=-1, keepdims=True))
        p = (e / jnp.sum(e, axis=-1, keepdims=True)).astype(BF16)
        o_ref[:, sl] = jnp.dot(p, v_ref[:, sl].astype(BF16), preferred_element_type=F32)


def _mem_attn(z, g, k_arr, k_spec, v_arr, v_spec, *, n_batch, t_len, tt, row_off, out_rows, alias=None):
    mem_w = g.shape[1] * MEM_HEADS
    col_blk = z.shape[1] // mem_w - 1
    nt = t_len // tt
    body, extra, extra_specs, kw = _with_alias(_mem_attn_body, 4, alias)
    return pl.pallas_call(
        body,
        out_shape=jax.ShapeDtypeStruct((out_rows, mem_w), F32),
        grid=(n_batch, nt),
        in_specs=[
            pl.BlockSpec((tt, mem_w), lambda b, t: ((row_off + b * t_len) // tt + t, col_blk)),
            pl.BlockSpec((1, g.shape[1]), lambda b, t: (0, 0)),
            k_spec,
            v_spec,
        ] + extra_specs,
        out_specs=pl.BlockSpec((tt, mem_w), lambda b, t: ((row_off + b * t_len) // tt + t, 0)),
        compiler_params=_params("parallel", "arbitrary"),
        name="mem_attn",
        **kw,
    )(z, g, k_arr, v_arr, *extra)


def _out_proj_body(tok_ref, mem_ref, w_ref, x_ref, o_ref, cat_ref):
    tok_w = tok_ref.shape[1]

    @pl.when(pl.program_id(1) == 0)
    def _():
        cat_ref[:, :tok_w] = tok_ref[...].astype(cat_ref.dtype)
        cat_ref[:, tok_w:] = mem_ref[...].astype(cat_ref.dtype)

    o_ref[...] = x_ref[...] + _matmul(cat_ref[...], w_ref[...])


def _out_proj(tok, mem, w, x, *, layer, tm, tn, n_rows=None, in_blk_off=0, x_blk_off=0, out_rows=None, alias=None):
    d = x.shape[1]
    m = x.shape[0] if n_rows is None else n_rows
    tok_w, mem_w = tok.shape[1], mem.shape[1]
    out_rows = m if out_rows is None else out_rows
    body, extra, extra_specs, kw = _with_alias(_out_proj_body, 4, alias)
    return pl.pallas_call(
        body,
        out_shape=jax.ShapeDtypeStruct((out_rows, d), F32),
        grid=(pl.cdiv(m, tm), d // tn),
        in_specs=[
            _row_block_spec((tm, tok_w), lambda i, j: (i + in_blk_off, 0), tm),
            _row_block_spec((tm, mem_w), lambda i, j: (i + in_blk_off, 0), tm),
            pl.BlockSpec((None, tok_w + mem_w, tn), lambda i, j: (layer, 0, j)),
            pl.BlockSpec((tm, tn), lambda i, j: (i + x_blk_off, j)),
        ] + extra_specs,
        out_specs=pl.BlockSpec((tm, tn), lambda i, j: (i + in_blk_off, j)),
        scratch_shapes=[pltpu.VMEM((tm, tok_w + mem_w), w.dtype)],
        compiler_params=_params("parallel", "arbitrary"),
        name="out_proj",
        **kw,
    )(tok, mem, w, x, *extra)


def _route_body(x_ref, g_ref, wh_ref, wl_ref, b_ref, info_ref, *, n_groups, per_group):
    h = _rms(x_ref[...], g_ref[...])
    hh, hl = _split_bf16(h)
    d = lambda a, b: jnp.dot(a, b, preferred_element_type=F32)
    logits = d(hh, wh_ref[...]) + (d(hh, wl_ref[...]) + d(hl, wh_ref[...])) + b_ref[...]
    lane = lax.broadcasted_iota(jnp.int32, logits.shape, 1)
    big = jnp.int32(ROUTE_LANES)

    def first_max(mask):
        v = jnp.max(jnp.where(mask, logits, NEG), axis=-1, keepdims=True)
        i = jnp.min(jnp.where(mask & (logits == v), lane, big), axis=-1, keepdims=True)
        return v, i

    g_mask = lane < n_groups
    g_max, g_sel = first_max(g_mask)
    p_group = 1.0 / jnp.sum(jnp.where(g_mask, jnp.exp(logits - g_max), 0.0), axis=-1, keepdims=True)
    lo = n_groups + g_sel * per_group
    e_mask = (lane >= lo) & (lane < lo + per_group)
    v1, i1 = first_max(e_mask)
    v2, i2 = first_max(e_mask & (lane != i1))
    e2 = jnp.exp(v2 - v1)
    w1 = p_group / (1.0 + e2)
    w2 = p_group * e2 / (1.0 + e2)
    id1 = (i1 - n_groups).astype(F32)
    id2 = (i2 - n_groups).astype(F32)
    info_ref[...] = jnp.where(lane == 0, id1, jnp.where(lane == 1, id2,
                              jnp.where(lane == 2, w1, jnp.where(lane == 3, w2, 0.0))))


def _route(x, g, w_hi, w_lo, bias, *, layer, tm, n_groups, per_group):
    m, d = x.shape
    body = functools.partial(_route_body, n_groups=n_groups, per_group=per_group)
    return pl.pallas_call(
        body,
        out_shape=jax.ShapeDtypeStruct((m, ROUTE_LANES), F32),
        grid=(pl.cdiv(m, tm),),
        in_specs=[
            pl.BlockSpec((tm, d), lambda i: (i, 0)),
            pl.BlockSpec((1, d), lambda i: (0, 0)),
            pl.BlockSpec((None, d, ROUTE_LANES), lambda i: (layer, 0, 0)),
            pl.BlockSpec((None, d, ROUTE_LANES), lambda i: (layer, 0, 0)),
            pl.BlockSpec((None, 1, ROUTE_LANES), lambda i: (layer, 0, 0)),
        ],
        out_specs=pl.BlockSpec((tm, ROUTE_LANES), lambda i: (i, 0)),
        compiler_params=_params("parallel"),
        name="moe_route",
    )(x, g, w_hi, w_lo, bias)


def _moe_plan(info, n_experts, tm, sub, n_pad_pairs):
    m = info.shape[0]
    ids = info[:, :EXPERT_TOPK].astype(jnp.int32).reshape(-1)
    n_pairs = m * EXPERT_TOPK
    n_tiles = n_pairs // tm + n_experts
    n_slots = n_tiles * tm
    onehot = (ids[:, None] == jnp.arange(n_experts)[None, :]).astype(jnp.int32)
    csum = jnp.cumsum(onehot, axis=0)
    rank = jnp.sum(csum * onehot, axis=1) - 1
    counts = csum[-1]
    tiles_e = (counts + tm - 1) // tm
    tile_end = jnp.cumsum(tiles_e)
    tile_start = tile_end - tiles_e
    slot_of_pair = jnp.sum(onehot * tile_start[None, :], axis=1) * tm + rank
    total = tile_end[-1]
    t_idx = jnp.arange(n_tiles, dtype=jnp.int32)
    t_eff = jnp.minimum(t_idx, total - 1).astype(jnp.int32)
    tile_expert = jnp.sum((tile_end[None, :] <= t_eff[:, None]).astype(jnp.int32), axis=1)
    te_hot = (tile_expert[:, None] == jnp.arange(n_experts)[None, :]).astype(jnp.int32)
    rows_left = jnp.sum(te_hot * (counts[None, :] - (t_eff[:, None] - tile_start[None, :]) * tm), axis=1)
    tile_subs = jnp.where(t_idx < total, (jnp.clip(rows_left, 0, tm) + sub - 1) // sub, 0).astype(jnp.int32)
    pair = jnp.arange(n_pairs, dtype=jnp.int32)
    slot_src = jnp.zeros((n_slots,), jnp.int32).at[slot_of_pair].set(pair // EXPERT_TOPK, unique_indices=True)
    pair_slot = jnp.concatenate([slot_of_pair.astype(jnp.int32), jnp.zeros((n_pad_pairs - n_pairs,), jnp.int32)])
    return slot_src, pair_slot, tile_expert.astype(jnp.int32), t_eff, tile_subs


def _gate_up_body(te_ref, tb_ref, ts_ref, src_ref, x_hbm, g_ref, wg_ref, wu_ref, act_ref, xbuf, xn_ref, sem,
                  *, tm, sub, n_tiles):
    t, c = pl.program_id(0), pl.program_id(1)
    nsub = tm // sub
    n_here = ts_ref[t]

    def start_sub(tile, sb):
        def issue(i, carry):
            r = sb * sub + i
            tok = src_ref[tile * tm + r]
            pltpu.make_async_copy(x_hbm.at[pl.ds(tok, 1)], xbuf.at[pl.ds(r, 1)], sem.at[0]).start()
            return carry
        lax.fori_loop(0, sub, issue, 0, unroll=DMA_UNROLL)

    @pl.when((t == 0) & (c == 0))
    def _():
        for sb in range(nsub):
            pl.when(sb < n_here)(functools.partial(start_sub, 0, sb))

    @pl.when((c == 0) & (n_here > 0))
    def _():
        for sb in range(nsub):
            @pl.when(sb < n_here)
            def _():
                pltpu.make_async_copy(x_hbm.at[pl.ds(0, sub)], xbuf.at[pl.ds(0, sub)], sem.at[0]).wait()
        for sb in range(nsub):
            @pl.when(sb < n_here)
            def _():
                rows = slice(sb * sub, (sb + 1) * sub)
                xn_ref[rows, :] = _rms(xbuf[rows, :], g_ref[...]).astype(BF16)

    nxt = jnp.minimum(t + 1, n_tiles - 1)

    @pl.when((t + 1 < n_tiles) & (c < ts_ref[nxt]))
    def _():
        start_sub(t + 1, c)

    for v in range(1, nsub + 1):
        @pl.when(n_here == v)
        def _():
            x = xn_ref[:v * sub, :]
            gate = jnp.dot(x, wg_ref[...].astype(BF16), preferred_element_type=F32)
            up = jnp.dot(x, wu_ref[...].astype(BF16), preferred_element_type=F32)
            act_ref[:v * sub, :] = (gate * jax.nn.sigmoid(gate) * up).astype(BF16)


def _gate_up(x, g, w_gate_up, slot_src, tile_expert, tile_block, tile_subs, *, layer, tm, sub):
    d = x.shape[1]
    f = w_gate_up.shape[3] // 2
    nc = tm // sub
    fc = f // nc
    n_tiles = tile_expert.shape[0]
    cc = lambda t, c, ts: jnp.where(ts[t] > 0, c, nc - 1)
    body = functools.partial(_gate_up_body, tm=tm, sub=sub, n_tiles=n_tiles)
    grid_spec = pltpu.PrefetchScalarGridSpec(
        num_scalar_prefetch=4,
        grid=(n_tiles, nc),
        in_specs=[
            pl.BlockSpec(memory_space=pl.ANY),
            pl.BlockSpec((1, d), lambda t, c, te, tb, ts, src: (0, 0)),
            pl.BlockSpec((None, None, d, fc), lambda t, c, te, tb, ts, src: (layer, te[t], 0, cc(t, c, ts))),
            pl.BlockSpec((None, None, d, fc), lambda t, c, te, tb, ts, src: (layer, te[t], 0, nc + cc(t, c, ts))),
        ],
        out_specs=pl.BlockSpec((tm, fc), lambda t, c, te, tb, ts, src: (tb[t], cc(t, c, ts))),
        scratch_shapes=[pltpu.VMEM((tm, d), F32), pltpu.VMEM((tm, d), BF16), pltpu.SemaphoreType.DMA((1,))],
    )
    return pl.pallas_call(
        body,
        out_shape=jax.ShapeDtypeStruct((n_tiles * tm, f), BF16),
        grid_spec=grid_spec,
        compiler_params=_params("arbitrary", "arbitrary", gather=True),
        name="moe_gate_up",
    )(tile_expert, tile_block, tile_subs, slot_src, x, g, w_gate_up, w_gate_up)


def _down_body(te_ref, tb_ref, ts_ref, act_ref, wd_ref, o_ref, *, sub):
    n_here = ts_ref[pl.program_id(0)]
    for v in range(1, act_ref.shape[0] // sub + 1):
        @pl.when(n_here == v)
        def _():
            o_ref[:v * sub, :] = jnp.dot(act_ref[:v * sub, :], wd_ref[...].astype(BF16), preferred_element_type=F32)


def _down(act, w_down, tile_expert, tile_block, tile_subs, *, layer, tm, tn, sub):
    f, d = w_down.shape[2:]
    nj = d // tn
    n_tiles = tile_expert.shape[0]
    jc = lambda t, j, ts: jnp.where(ts[t] > 0, j, nj - 1)
    grid_spec = pltpu.PrefetchScalarGridSpec(
        num_scalar_prefetch=3,
        grid=(n_tiles, nj),
        in_specs=[
            pl.BlockSpec((tm, f), lambda t, j, te, tb, ts: (tb[t], 0)),
            pl.BlockSpec((None, None, f, tn), lambda t, j, te, tb, ts: (layer, te[t], 0, jc(t, j, ts))),
        ],
        out_specs=pl.BlockSpec((tm, tn), lambda t, j, te, tb, ts: (tb[t], jc(t, j, ts))),
    )
    return pl.pallas_call(
        functools.partial(_down_body, sub=sub),
        out_shape=jax.ShapeDtypeStruct((n_tiles * tm, d), F32),
        grid_spec=grid_spec,
        compiler_params=_params("arbitrary", "arbitrary"),
        name="moe_down",
    )(tile_expert, tile_block, tile_subs, act, w_down)


def _combine_body(ps_ref, x_ref, info_ref, ys_hbm, o_ref, gbuf, sem, *, tm, row_off, n_steps):
    i = pl.program_id(0)
    slot = i % 2

    def start_tile(step, buf):
        def issue(r, carry):
            for k in range(EXPERT_TOPK):
                s = ps_ref[(row_off + step * tm + r) * EXPERT_TOPK + k]
                pltpu.make_async_copy(ys_hbm.at[pl.ds(s, 1)], gbuf.at[buf, k, pl.ds(r, 1)], sem.at[buf]).start()
            return carry
        lax.fori_loop(0, tm, issue, 0, unroll=DMA_UNROLL // EXPERT_TOPK)

    @pl.when(i == 0)
    def _():
        start_tile(0, 0)

    @pl.when(i + 1 < n_steps)
    def _():
        start_tile(i + 1, 1 - slot)

    for k in range(EXPERT_TOPK):
        pltpu.make_async_copy(ys_hbm.at[pl.ds(0, tm)], gbuf.at[slot, k], sem.at[slot]).wait()
    w = info_ref[...]
    y = x_ref[...]
    for k in range(EXPERT_TOPK):
        y = y + w[:, EXPERT_TOPK + k:EXPERT_TOPK + k + 1] * gbuf[slot, k]
    o_ref[...] = y


def _combine(x, info, ys, pair_slot, *, tm, row_blk_off=0, n_rows=None):
    d = x.shape[1]
    n_rows = x.shape[0] if n_rows is None else n_rows
    n_steps = pl.cdiv(n_rows, tm)
    body = functools.partial(_combine_body, tm=tm, row_off=row_blk_off * tm, n_steps=n_steps)
    grid_spec = pltpu.PrefetchScalarGridSpec(
        num_scalar_prefetch=1,
        grid=(n_steps,),
        in_specs=[
            pl.BlockSpec((tm, d), lambda i, ps: (i + row_blk_off, 0)),
            pl.BlockSpec((tm, ROUTE_LANES), lambda i, ps: (i + row_blk_off, 0)),
            pl.BlockSpec(memory_space=pl.ANY),
        ],
        out_specs=pl.BlockSpec((tm, d), lambda i, ps: (i, 0)),
        scratch_shapes=[pltpu.VMEM((2, EXPERT_TOPK, tm, d), F32), pltpu.SemaphoreType.DMA((2,))],
    )
    return pl.pallas_call(
        body,
        out_shape=jax.ShapeDtypeStruct((n_rows, d), F32),
        grid_spec=grid_spec,
        compiler_params=_params("arbitrary", gather=True),
        name="moe_combine",
    )(pair_slot, x, info, ys)


def _moe_experts(x, g, w_rt_hi, w_rt_lo, b_rt, w_gate_up, w_down, *, layer, n_groups, per_group, tm_route, tm, sub,
                 tn, tm_c):
    m = x.shape[0]
    info = _route(x, g, w_rt_hi, w_rt_lo, b_rt, layer=layer, tm=tm_route, n_groups=n_groups, per_group=per_group)
    n_pad_pairs = pl.cdiv(m, tm_c) * tm_c * EXPERT_TOPK
    slot_src, pair_slot, te, tb, ts = _moe_plan(info, n_groups * per_group, tm, sub, n_pad_pairs)
    act = _gate_up(x, g, w_gate_up, slot_src, te, tb, ts, layer=layer, tm=tm, sub=sub)
    ys = _down(act, w_down, te, tb, ts, layer=layer, tm=tm, tn=tn, sub=sub)
    return info, ys, pair_slot


def _moba_prompt_body(slopes_ref, q_ref, gq_ref, k_ref, v_ref, o_ref, *, kv_group, blk):
    g, own = pl.program_id(1), pl.program_id(2)
    hd = k_ref.shape[1]
    nb = k_ref.shape[0] // blk
    cols = kv_group * blk
    q = q_ref[...]
    qn = jnp.concatenate([_rms(q[:, r * hd:(r + 1) * hd], gq_ref[...]) for r in range(kv_group)], axis=0)
    qb = qn.astype(BF16)
    qcol = lax.broadcasted_iota(jnp.int32, (1, cols), 1)
    slope = jnp.zeros((1, cols), F32)
    for r in range(kv_group):
        slope = jnp.where(qcol // blk == r, slopes_ref[g * kv_group + r], slope)
    tq = qcol % blk
    key = lax.broadcasted_iota(jnp.int32, (blk, 1), 0)
    rel = slope * (tq - key).astype(F32) * LOG2E
    scale = hd ** -0.5 * LOG2E

    nbp = -(-nb // 8) * 8
    km = jnp.concatenate([jnp.mean(k_ref[n * blk:(n + 1) * blk, :], axis=0, keepdims=True) for n in range(nb)]
                         + [jnp.zeros((1, hd), F32)] * (nbp - nb), axis=0)
    blk_id = lax.broadcasted_iota(jnp.int32, (nbp, cols), 0)
    gate = jnp.where(blk_id < own, _dot3(km, qn, _NT), NEG)
    cnt = jnp.zeros(gate.shape, jnp.int32)
    for m_ in range(nb):
        gm = gate[m_:m_ + 1, :]
        ahead = jnp.where(gm > gate, 1, jnp.where(gm == gate, jnp.where(m_ < blk_id, 1, 0), 0))
        cnt = cnt + ahead
    skip = jnp.where(cnt < MOBA_TOPK, jnp.where(blk_id < own, 0.0, -NEG), -NEG)
    pen = skip + slope * ((own - blk_id) * blk).astype(F32) * LOG2E

    def block_scores(n):
        start = pl.multiple_of(n * blk, blk)
        kb = k_ref[pl.ds(start, blk), :].astype(BF16)
        vt = v_ref[pl.ds(start, blk), :].T.astype(BF16)
        s = lax.dot_general(kb, qb, _NT, preferred_element_type=F32) * scale
        return s - rel, vt

    s, vt = block_scores(own)
    s = jnp.where(key <= tq, s, NEG)
    m0 = jnp.max(s, axis=0, keepdims=True)
    p = jnp.exp2(s - m0)
    l0 = jnp.sum(p, axis=0, keepdims=True)
    a0 = jnp.dot(vt, p.astype(BF16), preferred_element_type=F32)

    def step(n, carry):
        m_run, l_run, acc = carry
        s, vt = block_scores(n)
        s = s - jnp.sum(jnp.where(blk_id == n, pen, 0.0), axis=0, keepdims=True)
        m_new = jnp.maximum(m_run, jnp.max(s, axis=0, keepdims=True))
        alpha = jnp.exp2(m_run - m_new)
        p = jnp.exp2(s - m_new)
        l_new = alpha * l_run + jnp.sum(p, axis=0, keepdims=True)
        acc = alpha * acc + jnp.dot(vt, p.astype(BF16), preferred_element_type=F32)
        return m_new, l_new, acc

    _, l_fin, acc = lax.fori_loop(0, own, step, (m0, l0, a0))
    out = (acc / l_fin).T
    for r in range(kv_group):
        o_ref[:, r * hd:(r + 1) * hd] = out[r * blk:(r + 1) * blk]


def _moba_prompt(z, gq, kv, slopes, *, n_batch, t_len, n_kv, kv_group, out_rows):
    hd = gq.shape[1]
    blk = MOBA_BLOCK
    gw = kv_group * hd
    nt = t_len // blk
    body = functools.partial(_moba_prompt_body, kv_group=kv_group, blk=blk)
    grid_spec = pltpu.PrefetchScalarGridSpec(
        num_scalar_prefetch=1,
        grid=(n_batch, n_kv, nt),
        in_specs=[
            pl.BlockSpec((blk, gw), lambda b, g, t, s: (b * nt + t, g)),
            pl.BlockSpec((1, hd), lambda b, g, t, s: (0, 0)),
            pl.BlockSpec((t_len, hd), lambda b, g, t, s: (b, g)),
            pl.BlockSpec((t_len, hd), lambda b, g, t, s: (b, n_kv + g)),
        ],
        out_specs=pl.BlockSpec((blk, gw), lambda b, g, t, s: (b * nt + t, g)),
    )
    return pl.pallas_call(
        body,
        out_shape=jax.ShapeDtypeStruct((out_rows, n_kv * gw), F32),
        grid_spec=grid_spec,
        compiler_params=_params("parallel", "parallel", "arbitrary"),
        name="moba_prompt",
    )(slopes, z, gq, kv, kv)


def _sample_queries(q, gq, n_heads, hd):
    return jnp.concatenate([_rms(q[:, h * hd:(h + 1) * hd], gq) for h in range(n_heads)], axis=0)


def _row_slopes(slopes_ref, n_heads, t_len):
    row = lax.broadcasted_iota(jnp.int32, (n_heads * t_len, 1), 0)
    slope = jnp.zeros((n_heads * t_len, 1), F32)
    for h in range(n_heads):
        slope = jnp.where(row // t_len == h, slopes_ref[h], slope)
    return slope


def _moba_partial_body(pt_ref, slopes_ref, q_ref, gq_ref, k0_ref, k1_ref, v0_ref, v1_ref, po_ref, ps_ref,
                       qn_ref, bias_ref, *, n_kv, kv_group, past_len):
    n = pl.program_id(1)
    hd = gq_ref.shape[1]
    page = k0_ref.shape[0] // n_kv
    t_len = q_ref.shape[0]
    n_heads = n_kv * kv_group
    rows = n_heads * t_len
    grows = kv_group * t_len
    blk = 2 * page
    cols = blk * n_kv
    row = lax.broadcasted_iota(jnp.int32, (rows, 1), 0)

    @pl.when(n == 0)
    def _():
        qn_ref[...] = _sample_queries(q_ref[...], gq_ref[...], n_heads, hd)
        col = lax.broadcasted_iota(jnp.int32, (1, cols), 1)
        same_head = row // grows == col % n_kv
        slope = _row_slopes(slopes_ref, n_heads, t_len)
        bias_ref[...] = jnp.where(same_head, slope * (past_len + row % t_len - col // n_kv).astype(F32), -NEG) * LOG2E

    qn = qn_ref[...]
    kb = jnp.concatenate([k0_ref[...], k1_ref[...]], axis=0)
    vb = jnp.concatenate([v0_ref[...], v1_ref[...]], axis=0)
    s = lax.dot_general(qn.astype(BF16), kb.astype(BF16), _NT, preferred_element_type=F32) * (hd ** -0.5 * LOG2E)
    s = s - bias_ref[...]
    mx = jnp.max(s, axis=-1, keepdims=True)
    p = jnp.exp2(s - mx)
    den = jnp.sum(p, axis=-1, keepdims=True)
    po_ref[...] = jnp.dot(p.astype(BF16), vb.astype(BF16), preferred_element_type=F32)
    km = jnp.sum(kb.reshape(blk, n_kv, hd), axis=0) / blk
    km_rows = jnp.concatenate([jnp.broadcast_to(km[g:g + 1], (grows, hd)) for g in range(n_kv)], axis=0)
    gate = jnp.sum(qn * km_rows, axis=-1, keepdims=True)
    mx = mx / LOG2E + _row_slopes(slopes_ref, n_heads, t_len) * (n * blk).astype(F32)
    lane = lax.broadcasted_iota(jnp.int32, (rows, ROUTE_LANES), 1)
    ps_ref[...] = jnp.where(lane == 0, mx, jnp.where(lane == 1, den, jnp.where(lane == 2, gate, 0.0)))


def _moba_partials(z, gq, cache_k, cache_v, page_table, slopes, *, n_batch, t_len, row_off, kv_group, past_len):
    page, n_kv, hd = cache_k.shape[1:]
    pages_per_blk = MOBA_BLOCK // page
    assert pages_per_blk == 2
    n_pages = page_table.shape[1]
    nblk = n_pages // pages_per_blk
    n_heads = n_kv * kv_group
    rows = n_heads * t_len
    body = functools.partial(_moba_partial_body, n_kv=n_kv, kv_group=kv_group, past_len=past_len)
    cache_k = cache_k.reshape(cache_k.shape[0], page * n_kv, hd)
    cache_v = cache_v.reshape(cache_v.shape[0], page * n_kv, hd)
    page_spec = lambda which: pl.BlockSpec(
        (None, page * n_kv, hd), lambda b, n, pt, s: (pt[b * n_pages + n * pages_per_blk + which], 0, 0))
    grid_spec = pltpu.PrefetchScalarGridSpec(
        num_scalar_prefetch=2,
        grid=(n_batch, nblk),
        in_specs=[
            pl.BlockSpec((t_len, n_heads * hd), lambda b, n, pt, s: (row_off // t_len + b, 0)),
            pl.BlockSpec((1, hd), lambda b, n, pt, s: (0, 0)),
            page_spec(0), page_spec(1), page_spec(0), page_spec(1),
        ],
        out_specs=(
            pl.BlockSpec((None, None, rows, hd), lambda b, n, pt, s: (b, n, 0, 0)),
            pl.BlockSpec((None, None, rows, ROUTE_LANES), lambda b, n, pt, s: (b, n, 0, 0)),
        ),
        scratch_shapes=[pltpu.VMEM((rows, hd), F32), pltpu.VMEM((rows, MOBA_BLOCK * n_kv), F32)],
    )
    return pl.pallas_call(
        body,
        out_shape=(jax.ShapeDtypeStruct((n_batch, nblk, rows, hd), F32),
                   jax.ShapeDtypeStruct((n_batch, nblk, rows, ROUTE_LANES), F32)),
        grid_spec=grid_spec,
        compiler_params=_params("parallel", "arbitrary"),
        name="moba_partials",
    )(page_table.reshape(-1), slopes, z, gq, cache_k, cache_k, cache_v, cache_v)


def _moba_merge_body(slopes_ref, q_ref, gq_ref, kn_ref, vn_ref, po_ref, ps_ref, o_ref, *, n_kv, kv_group):
    hd = gq_ref.shape[1]
    t_len = q_ref.shape[0]
    n_heads = n_kv * kv_group
    rows = n_heads * t_len
    grows = kv_group * t_len
    nblk = ps_ref.shape[0]
    lane = lax.broadcasted_iota(jnp.int32, (rows, ROUTE_LANES), 1)
    m_blk = jnp.full((rows, ROUTE_LANES), NEG, F32)
    l_blk = jnp.zeros((rows, ROUTE_LANES), F32)
    gates = jnp.full((rows, ROUTE_LANES), NEG, F32)
    for n in range(nblk):
        st = ps_ref[n]
        m_blk = jnp.where(lane == n, st[:, 0:1], m_blk)
        l_blk = jnp.where(lane == n, st[:, 1:2], l_blk)
        gates = jnp.where(lane == n, st[:, 2:3], gates)
    w_blk = jnp.zeros((rows, ROUTE_LANES), F32)
    for _ in range(min(MOBA_TOPK, nblk)):
        best = jnp.max(gates, axis=-1, keepdims=True)
        first = jnp.min(jnp.where(gates == best, lane, ROUTE_LANES), axis=-1, keepdims=True)
        w_blk = jnp.where(lane == first, 1.0, w_blk)
        gates = jnp.where(lane == first, NEG, gates)
    m_blk = jnp.where(w_blk > 0.0, m_blk, NEG)
    m_past = jnp.max(m_blk, axis=-1, keepdims=True)

    qn = _sample_queries(q_ref[...], gq_ref[...], n_heads, hd)
    slope = _row_slopes(slopes_ref, n_heads, t_len)
    row = lax.broadcasted_iota(jnp.int32, (rows, 1), 0)
    tq = row % t_len
    tk = lax.broadcasted_iota(jnp.int32, (1, t_len), 1)
    s_own = jnp.concatenate([
        lax.dot_general(qn[g * grows:(g + 1) * grows].astype(BF16), kn_ref[:, g * hd:(g + 1) * hd].astype(BF16), _NT,
                        preferred_element_type=F32) for g in range(n_kv)], axis=0) * (hd ** -0.5)
    s_own = jnp.where(tk <= tq, s_own - slope * (tq - tk).astype(F32), NEG)
    m_all = jnp.maximum(m_past, jnp.max(s_own, axis=-1, keepdims=True))
    w_blk = jnp.where(w_blk > 0.0, jnp.exp(m_blk - m_all), 0.0)
    p_own = jnp.exp(s_own - m_all)
    den = jnp.sum(p_own, axis=-1, keepdims=True) + jnp.sum(w_blk * l_blk, axis=-1, keepdims=True)
    num = jnp.zeros((rows, hd), F32)
    for n in range(nblk):
        num = num + w_blk[:, n:n + 1] * po_ref[n]
    num = num + jnp.concatenate([
        jnp.dot(p_own[g * grows:(g + 1) * grows].astype(BF16), vn_ref[:, g * hd:(g + 1) * hd].astype(BF16),
                preferred_element_type=F32) for g in range(n_kv)], axis=0)
    out = num / den
    for h in range(n_heads):
        o_ref[:, h * hd:(h + 1) * hd] = out[h * t_len:(h + 1) * t_len]


def _moba_merge(z, gq, kv, part_o, part_s, slopes, *, n_batch, t_len, row_off, n_kv, kv_group, alias):
    hd = gq.shape[1]
    kw = n_kv * hd
    nblk, rows = part_o.shape[1:3]
    body = functools.partial(_moba_merge_body, n_kv=n_kv, kv_group=kv_group)
    wrapped = lambda *refs: body(*refs[:7], *refs[8:])
    new_rows = lambda b, s: (row_off // t_len + b, 0)
    grid_spec = pltpu.PrefetchScalarGridSpec(
        num_scalar_prefetch=1,
        grid=(n_batch,),
        in_specs=[
            pl.BlockSpec((t_len, kv_group * kw), new_rows),
            pl.BlockSpec((1, hd), lambda b, s: (0, 0)),
            pl.BlockSpec((t_len, kw), new_rows),
            pl.BlockSpec((t_len, kw), lambda b, s: (row_off // t_len + b, 1)),
            pl.BlockSpec((None, nblk, rows, hd), lambda b, s: (b, 0, 0, 0)),
            pl.BlockSpec((None, nblk, rows, ROUTE_LANES), lambda b, s: (b, 0, 0, 0)),
            pl.BlockSpec(memory_space=pl.ANY),
        ],
        out_specs=pl.BlockSpec((t_len, kv_group * kw), new_rows),
    )
    return pl.pallas_call(
        wrapped,
        out_shape=jax.ShapeDtypeStruct(alias.shape, F32),
        grid_spec=grid_spec,
        input_output_aliases={7: 0},
        compiler_params=_params("parallel"),
        name="moba_merge",
    )(slopes, z, gq, kv, kv, part_o, part_s, alias)


def _tile(m, pref):
    return pref if m >= pref else m


def kernel(x_prompt, x_sample, mem_prompt, cache_k, cache_v, cache_mem_k, cache_mem_v, state_pool, page_table, g_mix_norm, w_in, w_out, w_pool, pool_scale, g_q, g_kv_norm, w_kv, g_k, g_mem_norm, w_mem_kv, g_mem_q, g_mem_k, g_ffn_norm, w_group, b_group, w_router, b_router, w_gate_up, w_down):
    bp, tp, d = x_prompt.shape
    bs, ts, _ = x_sample.shape
    depth = w_in.shape[0]
    n_a = w_pool.shape[0]
    mem_tokens = mem_prompt.shape[1]
    mem_w = d // 4
    mem_hd = mem_w // MEM_HEADS
    tok_w = d - mem_w
    page, n_kv, hd = cache_k.shape[1:]
    kv_w = n_kv * hd
    n_heads = tok_w // hd
    kv_group = n_heads // n_kv
    past_len = page_table.shape[1] * page
    n_groups = w_group.shape[2]
    n_experts = w_router.shape[2]
    per_group = n_experts // n_groups
    d_expert = w_down.shape[2]
    mp, ms = bp * tp, bs * ts
    m_all = mp + ms

    tm_p = _tile(mp, 512)
    tm_d = _tile(mp, BIG_ROW_TILE)
    tn = _tile(d, 1024)
    tn_o = _tile(d, 512)
    tn_s = _tile(d, 512)
    tt_p = _tile(tp, MOBA_BLOCK)
    tn_e = _tile(d, 2048)
    tm_e = 1024 if m_all >= 4096 else 128
    sub_e = tm_e // 4
    tm_c = _tile(mp, 256)

    row = lambda v: v.reshape(1, -1).astype(F32)
    w_in_b, w_out_b, w_pool_b = w_in.astype(BF16), w_out.astype(BF16), w_pool.astype(BF16)
    w_kv_b, w_mem_kv_b = w_kv.astype(BF16)[None], w_mem_kv.astype(BF16)
    w_rt = jnp.concatenate([w_group, w_router, jnp.zeros((depth, d, ROUTE_LANES - n_groups - n_experts), F32)], axis=-1)
    w_rt_hi = w_rt.astype(BF16)
    w_rt_lo = (w_rt - w_rt_hi.astype(F32)).astype(BF16)
    b_rt = jnp.concatenate([b_group, b_router, jnp.zeros((depth, ROUTE_LANES - n_groups - n_experts), F32)],
                           axis=-1)[:, None, :]
    slopes = jnp.exp2(-8.0 * jnp.arange(1, n_heads + 1, dtype=F32) / n_heads)

    xp = x_prompt.reshape(mp, d)
    xs_ = x_sample.reshape(ms, d)
    sample_blk = mp // ms

    memx = mem_prompt.reshape(bp * mem_tokens, d)
    mem_kv_p = [
        _norm_matmul(memx, row(g_mem_norm[l]), w_mem_kv_b, layer=l, n_out=2 * mem_w, tm=_tile(bp * mem_tokens, 512),
                     tn=mem_w, head_groups=(mem_hd, 0),
                     gh=jnp.concatenate([jnp.tile(row(g_mem_k[l]), (1, MEM_HEADS)), jnp.ones((1, mem_w), F32)], axis=1))
        for l in range(depth)]
    mem_kv_p3 = [a.reshape(bp, mem_tokens, 2 * mem_w) for a in mem_kv_p]
    cmk = cache_mem_k.reshape(depth, bs, mem_tokens, mem_w)
    cmv = cache_mem_v.reshape(depth, bs, mem_tokens, mem_w)

    def mem_attend(z, l):
        y = _mem_attn(z, row(g_mem_q[l]),
                      mem_kv_p3[l], pl.BlockSpec((None, mem_tokens, mem_w), lambda b, t: (b, 0, 0)),
                      mem_kv_p3[l], pl.BlockSpec((None, mem_tokens, mem_w), lambda b, t: (b, 0, 1)),
                      n_batch=bp, t_len=tp, tt=tt_p, row_off=0, out_rows=m_all)
        return _mem_attn(z, row(g_mem_q[l]),
                         cmk, pl.BlockSpec((None, None, mem_tokens, mem_w), lambda b, t: (l, b, 0, 0)),
                         cmv, pl.BlockSpec((None, None, mem_tokens, mem_w), lambda b, t: (l, b, 0, 0)),
                         n_batch=bs, t_len=ts, tt=ts, row_off=mp, out_rows=m_all, alias=y)

    def moe_experts(x, l):
        return _moe_experts(x, row(g_ffn_norm[l]), w_rt_hi, w_rt_lo, b_rt, w_gate_up, w_down, layer=l,
                            n_groups=n_groups, per_group=per_group, tm_route=tm_p, tm=tm_e, sub=sub_e, tn=tn_e,
                            tm_c=tm_c)

    x = None
    z_pool = []
    kv = None
    y_prompt = y_sample = None
    for l in range(depth):
        g_mix = row(g_mix_norm[l])
        if l == 0:
            z = _norm_matmul(xp, g_mix, w_in_b, layer=l, n_out=d, tm=tm_d, tn=tn_o, out_rows=m_all)
            z = _norm_matmul(xs_, g_mix, w_in, layer=l, n_out=d, tm=ms, tn=tn_s, out_rows=m_all,
                             row_blk_off=sample_blk, alias=z)
        else:
            z = _norm_matmul(x, g_mix, w_in_b, layer=l, n_out=d, n_rows=mp, tm=tm_d, tn=tn_o, out_rows=m_all)
            z = _norm_matmul(x, g_mix, w_in_b, layer=l, n_out=d, n_rows=ms, tm=ms, tn=tn, in_blk_off=sample_blk,
                             out_rows=m_all, row_blk_off=sample_blk, alias=z)
        mem_y = mem_attend(z, l)
        if l < n_a:
            z_pool.append(z)
            scale = row(pool_scale[l])
            zero_buf = jnp.zeros((bp, POOL_HALO, tok_w), F32)
            samp_buf = jnp.concatenate([jnp.zeros((bs, 1, tok_w), F32), state_pool[l]], axis=1)
            tok_y = _pool_mix(z, zero_buf, w_pool_b[l], scale, n_batch=bp, t_len=tp, tt=tt_p, row_off=0, pos0=0,
                              out_rows=m_all)
            tok_y = _pool_mix(z, samp_buf, w_pool[l], scale, n_batch=bs, t_len=ts, tt=ts, row_off=mp,
                              pos0=past_len, out_rows=m_all, alias=tok_y)
        else:
            if kv is None:
                gh_kv = jnp.concatenate([jnp.tile(row(g_k), (1, n_kv)), jnp.ones((1, kv_w), F32)], axis=1)
                kv = _norm_matmul(x, row(g_kv_norm), w_kv_b, layer=0, n_out=2 * kv_w, n_rows=mp, tm=tm_d,
                                  tn=kv_w // 2, head_groups=(hd, hd, 0, 0), gh=gh_kv, out_rows=m_all)
                kv = _norm_matmul(x, row(g_kv_norm), w_kv_b, layer=0, n_out=2 * kv_w, n_rows=ms, tm=ms, tn=kv_w,
                                  head_groups=(hd, 0), gh=gh_kv, in_blk_off=sample_blk, out_rows=m_all,
                                  row_blk_off=sample_blk, alias=kv)
            gq = row(g_q[l - n_a])
            tok_y = _moba_prompt(z, gq, kv, slopes, n_batch=bp, t_len=tp, n_kv=n_kv, kv_group=kv_group,
                                 out_rows=m_all)
            part_o, part_s = _moba_partials(z, gq, cache_k, cache_v, page_table, slopes, n_batch=bs, t_len=ts,
                                            row_off=mp, kv_group=kv_group, past_len=past_len)
            tok_y = _moba_merge(z, gq, kv, part_o, part_s, slopes, n_batch=bs, t_len=ts, row_off=mp,
                                n_kv=n_kv, kv_group=kv_group, alias=tok_y)
        if l == 0:
            x = _out_proj(tok_y, mem_y, w_out_b, xp, layer=l, tm=tm_d, tn=tn_o, out_rows=m_all)
            x = _out_proj(tok_y, mem_y, w_out, xs_, layer=l, tm=ms, tn=tn_s, in_blk_off=sample_blk, out_rows=m_all,
                          alias=x)
        else:
            x_att = _out_proj(tok_y, mem_y, w_out_b, x, layer=l, n_rows=mp, tm=tm_d, tn=tn_o, out_rows=m_all)
            x = _out_proj(tok_y, mem_y, w_out_b, x, layer=l, n_rows=ms, tm=ms, tn=tn, in_blk_off=sample_blk,
                          x_blk_off=sample_blk, out_rows=m_all, alias=x_att)
        info, ys, pair_slot = moe_experts(x, l)
        if l + 1 < depth:
            x = _combine(x, info, ys, pair_slot, tm=tm_c)
        else:
            y_prompt = _combine(x, info, ys, pair_slot, tm=tm_c, n_rows=mp)
            y_sample = _combine(x, info, ys, pair_slot, tm=ms, row_blk_off=sample_blk, n_rows=ms)

    y_prompt = y_prompt.reshape(bp, tp, d)
    y_sample = y_sample.reshape(bs, ts, d)
    k_prompt = kv[:mp, :kv_w].reshape(bp, tp, n_kv, hd)
    v_prompt = kv[:mp, kv_w:].reshape(bp, tp, n_kv, hd)
    k_sample = kv[mp:, :kv_w].reshape(bs, ts, n_kv, hd)
    v_sample = kv[mp:, kv_w:].reshape(bs, ts, n_kv, hd)
    mem_k_prompt = jnp.stack([a[:, :, :mem_w].reshape(bp, mem_tokens, MEM_HEADS, mem_hd) for a in mem_kv_p3])
    mem_v_prompt = jnp.stack([a[:, :, mem_w:].reshape(bp, mem_tokens, MEM_HEADS, mem_hd) for a in mem_kv_p3])
    pool_prompt = jnp.stack([zz[:mp].reshape(bp, tp, d)[:, tp - POOL_BUF:, :tok_w] for zz in z_pool])
    pool_sample = jnp.stack([
        jnp.concatenate([state_pool[i], zz[mp:].reshape(bs, ts, d)[:, :, :tok_w]], axis=1)[:, -POOL_BUF:]
        for i, zz in enumerate(z_pool)])
    return (y_prompt, y_sample, k_prompt, v_prompt, k_sample, v_sample,
            mem_k_prompt, mem_v_prompt, pool_prompt, pool_sample)
```

```python
import functools

import jax
import jax.numpy as jnp
from jax import lax
from jax.experimental import pallas as pl
from jax.experimental.pallas import tpu as pltpu

F32 = jnp.float32
BF16 = jnp.bfloat16

EPS = 1e-6
POOL_WINDOWS = (2, 4, 8, 16)
POOL_BUF = max(POOL_WINDOWS) - 1
POOL_HALO = POOL_BUF + 1
MEM_HEADS = 4
MOBA_BLOCK = 256
MOBA_TOPK = 3
EXPERT_TOPK = 2
ROUTE_LANES = 128
NEG = -1e30
LOG2E = 1.4426950408889634
VMEM_LIMIT = 60 * 1024 * 1024
DMA_UNROLL = 8
BIG_ROW_TILE = 1024
NORM_ROWS = 256

_NT = (((1,), (1,)), ((), ()))
_NN = (((1,), (0,)), ((), ()))


def _params(*sem, gather=False):
    return pltpu.CompilerParams(dimension_semantics=sem, vmem_limit_bytes=VMEM_LIMIT, disable_bounds_checks=gather)


def _split_bf16(a):
    hi = a.astype(BF16)
    lo = (a - hi.astype(F32)).astype(BF16)
    return hi, lo


def _dot3(a, b, dims):
    ah, al = _split_bf16(a)
    bh, bl = _split_bf16(b)
    d = lambda x, y: lax.dot_general(x, y, dims, preferred_element_type=F32)
    return d(ah, bh) + (d(ah, bl) + d(al, bh))


def _rms(x, g):
    return x * lax.rsqrt(jnp.mean(x * x, axis=-1, keepdims=True) + EPS) * g


def _matmul(a, w):
    if w.dtype == F32:
        return _dot3(a.astype(F32), w, _NN)
    return jnp.dot(a.astype(BF16), w, preferred_element_type=F32)


def _with_alias(body, n_in, alias):
    if alias is None:
        return body, [], [], {}
    wrapped = lambda *refs: body(*refs[:n_in], *refs[n_in + 1:])
    return wrapped, [alias], [pl.BlockSpec(memory_space=pl.ANY)], {"input_output_aliases": {n_in: 0}}


def _norm_matmul_body(x_ref, g_ref, w_ref, gh_ref, o_ref, hn_ref, *, head_groups):
    j = pl.program_id(1)

    @pl.when(j == 0)
    def _():
        tm = x_ref.shape[0]
        step = min(tm, NORM_ROWS)
        for r0 in range(0, tm, step):
            hn_ref[r0:r0 + step, :] = _rms(x_ref[r0:r0 + step, :], g_ref[...]).astype(hn_ref.dtype)

    acc = _matmul(hn_ref[...], w_ref[...])
    tn = acc.shape[1]

    def write(gs):
        if gs == 0:
            o_ref[...] = acc
            return
        for c in range(tn // gs):
            sl = slice(c * gs, (c + 1) * gs)
            o_ref[:, sl] = _rms(acc[:, sl], gh_ref[:, sl])

    if len(set(head_groups)) == 1:
        write(head_groups[0])
    else:
        for jj, gs in enumerate(head_groups):
            pl.when(j == jj)(functools.partial(write, gs))


def _row_block_spec(shape, index_map, rows):
    if rows >= BIG_ROW_TILE:
        return pl.BlockSpec(shape, index_map, pipeline_mode=pl.Buffered(1))
    return pl.BlockSpec(shape, index_map)


def _norm_matmul(x, g, w, *, layer, n_out, tm, tn, head_groups=None, gh=None,
                 n_rows=None, in_blk_off=0, out_rows=None, row_blk_off=0, alias=None):
    k = x.shape[1]
    m = x.shape[0] if n_rows is None else n_rows
    nj = n_out // tn
    head_groups = tuple(head_groups) if head_groups else (0,) * nj
    if gh is None:
        gh = jnp.ones((1, n_out), F32)
    out_rows = m if out_rows is None else out_rows
    body = functools.partial(_norm_matmul_body, head_groups=head_groups)
    body, extra, extra_specs, kw = _with_alias(body, 4, alias)
    return pl.pallas_call(
        body,
        out_shape=jax.ShapeDtypeStruct((out_rows, n_out), F32),
        grid=(pl.cdiv(m, tm), nj),
        in_specs=[
            _row_block_spec((tm, k), lambda i, j: (i + in_blk_off, 0), tm),
            pl.BlockSpec((1, k), lambda i, j: (0, 0)),
            pl.BlockSpec((None, k, tn), lambda i, j: (layer, 0, j)),
            pl.BlockSpec((1, tn), lambda i, j: (0, j)),
        ] + extra_specs,
        out_specs=pl.BlockSpec((tm, tn), lambda i, j: (i + row_blk_off, j)),
        scratch_shapes=[pltpu.VMEM((tm, k), w.dtype)],
        compiler_params=_params("parallel", "arbitrary"),
        name="norm_matmul",
        **kw,
    )(x, g, w, gh, *extra)


def _pool_body(cur_ref, halo_ref, buf_ref, w_ref, scale_ref, o_ref, *, tt, pos0):
    t = pl.program_id(1)
    cur = cur_ref[...]
    halo = jnp.where(t == 0, buf_ref[...], halo_ref[...])
    full = jnp.concatenate([halo, cur], axis=0)
    pos = pos0 + t * tt + lax.broadcasted_iota(jnp.int32, (tt, 1), 0)
    gw = cur.shape[1] // len(POOL_WINDOWS)
    for gi, win in enumerate(POOL_WINDOWS):
        sl = slice(gi * gw, (gi + 1) * gw)
        s = full[:, sl]
        sh = 1
        while sh < win:
            s = s + pltpu.roll(s, shift=sh, axis=0)
            sh *= 2
        cnt = jnp.minimum(pos + 1, win).astype(F32)
        d = s[POOL_HALO:] / cnt - cur[:, sl]
        y = _matmul(d, w_ref[gi])
        o_ref[:, sl] = y * scale_ref[:, sl]


def _pool_mix(z, buf16, w_pool, scale, *, n_batch, t_len, tt, row_off, pos0, out_rows, alias=None):
    tok_w = scale.shape[1]
    nt = t_len // tt
    cur_blk = lambda b, t: ((row_off + b * t_len) // tt + t, 0)
    halo_blk = lambda b, t: (jnp.maximum((row_off + b * t_len + t * tt) // POOL_HALO - 1, 0), 0)
    body = functools.partial(_pool_body, tt=tt, pos0=pos0)
    body, extra, extra_specs, kw = _with_alias(body, 5, alias)
    return pl.pallas_call(
        body,
        out_shape=jax.ShapeDtypeStruct((out_rows, tok_w), F32),
        grid=(n_batch, nt),
        in_specs=[
            pl.BlockSpec((tt, tok_w), cur_blk),
            pl.BlockSpec((POOL_HALO, tok_w), halo_blk),
            pl.BlockSpec((None, POOL_HALO, tok_w), lambda b, t: (b, 0, 0)),
            pl.BlockSpec(w_pool.shape, lambda b, t: (0, 0, 0)),
            pl.BlockSpec((1, tok_w), lambda b, t: (0, 0)),
        ] + extra_specs,
        out_specs=pl.BlockSpec((tt, tok_w), cur_blk),
        compiler_params=_params("parallel", "arbitrary"),
        name="pool_mix",
        **kw,
    )(z, z, buf16, w_pool, scale, *extra)


def _mem_attn_body(q_ref, g_ref, k_ref, v_ref, o_ref):
    q = q_ref[...]
    hd = q.shape[1] // MEM_HEADS
    for h in range(MEM_HEADS):
        sl = slice(h * hd, (h + 1) * hd)
        qh = _rms(q[:, sl], g_ref[...]).astype(BF16)
        s = lax.dot_general(qh, k_ref[:, sl].astype(BF16), _NT, preferred_element_type=F32) * (hd ** -0.5)
        e = jnp.exp(s - jnp.max(s, axis=-1, keepdims=True))
        p = (e / jnp.sum(e, axis=-1, keepdims=True)).astype(BF16)
        o_ref[:, sl] = jnp.dot(p, v_ref[:, sl].astype(BF16), preferred_element_type=F32)


def _mem_attn(z, g, k_arr, k_spec, v_arr, v_spec, *, n_batch, t_len, tt, row_off, out_rows, alias=None):
    mem_w = g.shape[1] * MEM_HEADS
    col_blk = z.shape[1] // mem_w - 1
    nt = t_len // tt
    body, extra, extra_specs, kw = _with_alias(_mem_attn_body, 4, alias)
    return pl.pallas_call(
        body,
        out_shape=jax.ShapeDtypeStruct((out_rows, mem_w), F32),
        grid=(n_batch, nt),
        in_specs=[
            pl.BlockSpec((tt, mem_w), lambda b, t: ((row_off + b * t_len) // tt + t, col_blk)),
            pl.BlockSpec((1, g.shape[1]), lambda b, t: (0, 0)),
            k_spec,
            v_spec,
        ] + extra_specs,
        out_specs=pl.BlockSpec((tt, mem_w), lambda b, t: ((row_off + b * t_len) // tt + t, 0)),
        compiler_params=_params("parallel", "arbitrary"),
        name="mem_attn",
        **kw,
    )(z, g, k_arr, v_arr, *extra)


def _out_proj_body(tok_ref, mem_ref, w_ref, x_ref, o_ref, cat_ref):
    tok_w = tok_ref.shape[1]

    @pl.when(pl.program_id(1) == 0)
    def _():
        cat_ref[:, :tok_w] = tok_ref[...].astype(cat_ref.dtype)
        cat_ref[:, tok_w:] = mem_ref[...].astype(cat_ref.dtype)

    o_ref[...] = x_ref[...] + _matmul(cat_ref[...], w_ref[...])


def _out_proj(tok, mem, w, x, *, layer, tm, tn, n_rows=None, in_blk_off=0, x_blk_off=0, out_rows=None, alias=None):
    d = x.shape[1]
    m = x.shape[0] if n_rows is None else n_rows
    tok_w, mem_w = tok.shape[1], mem.shape[1]
    out_rows = m if out_rows is None else out_rows
    body, extra, extra_specs, kw = _with_alias(_out_proj_body, 4, alias)
    return pl.pallas_call(
        body,
        out_shape=jax.ShapeDtypeStruct((out_rows, d), F32),
        grid=(pl.cdiv(m, tm), d // tn),
        in_specs=[
            _row_block_spec((tm, tok_w), lambda i, j: (i + in_blk_off, 0), tm),
            _row_block_spec((tm, mem_w), lambda i, j: (i + in_blk_off, 0), tm),
            pl.BlockSpec((None, tok_w + mem_w, tn), lambda i, j: (layer, 0, j)),
            pl.BlockSpec((tm, tn), lambda i, j: (i + x_blk_off, j)),
        ] + extra_specs,
        out_specs=pl.BlockSpec((tm, tn), lambda i, j: (i + in_blk_off, j)),
        scratch_shapes=[pltpu.VMEM((tm, tok_w + mem_w), w.dtype)],
        compiler_params=_params("parallel", "arbitrary"),
        name="out_proj",
        **kw,
    )(tok, mem, w, x, *extra)


def _route_body(x_ref, g_ref, wh_ref, wl_ref, b_ref, info_ref, *, n_groups, per_group):
    h = _rms(x_ref[...], g_ref[...])
    hh, hl = _split_bf16(h)
    d = lambda a, b: jnp.dot(a, b, preferred_element_type=F32)
    logits = d(hh, wh_ref[...]) + (d(hh, wl_ref[...]) + d(hl, wh_ref[...])) + b_ref[...]
    lane = lax.broadcasted_iota(jnp.int32, logits.shape, 1)
    big = jnp.int32(ROUTE_LANES)

    def first_max(mask):
        v = jnp.max(jnp.where(mask, logits, NEG), axis=-1, keepdims=True)
        i = jnp.min(jnp.where(mask & (logits == v), lane, big), axis=-1, keepdims=True)
        return v, i

    g_mask = lane < n_groups
    g_max, g_sel = first_max(g_mask)
    p_group = 1.0 / jnp.sum(jnp.where(g_mask, jnp.exp(logits - g_max), 0.0), axis=-1, keepdims=True)
    lo = n_groups + g_sel * per_group
    e_mask = (lane >= lo) & (lane < lo + per_group)
    v1, i1 = first_max(e_mask)
    v2, i2 = first_max(e_mask & (lane != i1))
    e2 = jnp.exp(v2 - v1)
    w1 = p_group / (1.0 + e2)
    w2 = p_group * e2 / (1.0 + e2)
    id1 = (i1 - n_groups).astype(F32)
    id2 = (i2 - n_groups).astype(F32)
    info_ref[...] = jnp.where(lane == 0, id1, jnp.where(lane == 1, id2,
                              jnp.where(lane == 2, w1, jnp.where(lane == 3, w2, 0.0))))


def _route(x, g, w_hi, w_lo, bias, *, layer, tm, n_groups, per_group):
    m, d = x.shape
    body = functools.partial(_route_body, n_groups=n_groups, per_group=per_group)
    return pl.pallas_call(
        body,
        out_shape=jax.ShapeDtypeStruct((m, ROUTE_LANES), F32),
        grid=(pl.cdiv(m, tm),),
        in_specs=[
            pl.BlockSpec((tm, d), lambda i: (i, 0)),
            pl.BlockSpec((1, d), lambda i: (0, 0)),
            pl.BlockSpec((None, d, ROUTE_LANES), lambda i: (layer, 0, 0)),
            pl.BlockSpec((None, d, ROUTE_LANES), lambda i: (layer, 0, 0)),
            pl.BlockSpec((None, 1, ROUTE_LANES), lambda i: (layer, 0, 0)),
        ],
        out_specs=pl.BlockSpec((tm, ROUTE_LANES), lambda i: (i, 0)),
        compiler_params=_params("parallel"),
        name="moe_route",
    )(x, g, w_hi, w_lo, bias)


def _moe_plan(info, n_experts, tm, sub, n_pad_pairs):
    m = info.shape[0]
    ids = info[:, :EXPERT_TOPK].astype(jnp.int32).reshape(-1)
    n_pairs = m * EXPERT_TOPK
    n_tiles = n_pairs // tm + n_experts
    n_slots = n_tiles * tm
    onehot = (ids[:, None] == jnp.arange(n_experts)[None, :]).astype(jnp.int32)
    csum = jnp.cumsum(onehot, axis=0)
    rank = jnp.sum(csum * onehot, axis=1) - 1
    counts = csum[-1]
    tiles_e = (counts + tm - 1) // tm
    tile_end = jnp.cumsum(tiles_e)
    tile_start = tile_end - tiles_e
    slot_of_pair = jnp.sum(onehot * tile_start[None, :], axis=1) * tm + rank
    total = tile_end[-1]
    t_idx = jnp.arange(n_tiles, dtype=jnp.int32)
    t_eff = jnp.minimum(t_idx, total - 1).astype(jnp.int32)
    tile_expert = jnp.sum((tile_end[None, :] <= t_eff[:, None]).astype(jnp.int32), axis=1)
    te_hot = (tile_expert[:, None] == jnp.arange(n_experts)[None, :]).astype(jnp.int32)
    rows_left = jnp.sum(te_hot * (counts[None, :] - (t_eff[:, None] - tile_start[None, :]) * tm), axis=1)
    tile_subs = jnp.where(t_idx < total, (jnp.clip(rows_left, 0, tm) + sub - 1) // sub, 0).astype(jnp.int32)
    pair = jnp.arange(n_pairs, dtype=jnp.int32)
    slot_src = jnp.zeros((n_slots,), jnp.int32).at[slot_of_pair].set(pair // EXPERT_TOPK, unique_indices=True)
    pair_slot = jnp.concatenate([slot_of_pair.astype(jnp.int32), jnp.zeros((n_pad_pairs - n_pairs,), jnp.int32)])
    return slot_src, pair_slot, tile_expert.astype(jnp.int32), t_eff, tile_subs


def _gate_up_body(te_ref, tb_ref, ts_ref, src_ref, x_hbm, g_ref, wg_ref, wu_ref, act_ref, xbuf, xn_ref, sem,
                  *, tm, sub, n_tiles):
    t, c = pl.program_id(0), pl.program_id(1)
    nsub = tm // sub
    n_here = ts_ref[t]

    def start_sub(tile, sb):
        def issue(i, carry):
            r = sb * sub + i
            tok = src_ref[tile * tm + r]
            pltpu.make_async_copy(x_hbm.at[pl.ds(tok, 1)], xbuf.at[pl.ds(r, 1)], sem.at[0]).start()
            return carry
        lax.fori_loop(0, sub, issue, 0, unroll=DMA_UNROLL)

    @pl.when((t == 0) & (c == 0))
    def _():
        for sb in range(nsub):
            pl.when(sb < n_here)(functools.partial(start_sub, 0, sb))

    @pl.when((c == 0) & (n_here > 0))
    def _():
        for sb in range(nsub):
            @pl.when(sb < n_here)
            def _():
                pltpu.make_async_copy(x_hbm.at[pl.ds(0, sub)], xbuf.at[pl.ds(0, sub)], sem.at[0]).wait()
        for sb in range(nsub):
            @pl.when(sb < n_here)
            def _():
                rows = slice(sb * sub, (sb + 1) * sub)
                xn_ref[rows, :] = _rms(xbuf[rows, :], g_ref[...]).astype(BF16)

    nxt = jnp.minimum(t + 1, n_tiles - 1)

    @pl.when((t + 1 < n_tiles) & (c < ts_ref[nxt]))
    def _():
        start_sub(t + 1, c)

    for v in range(1, nsub + 1):
        @pl.when(n_here == v)
        def _():
            x = xn_ref[:v * sub, :]
            gate = jnp.dot(x, wg_ref[...].astype(BF16), preferred_element_type=F32)
            up = jnp.dot(x, wu_ref[...].astype(BF16), preferred_element_type=F32)
            act_ref[:v * sub, :] = (gate * jax.nn.sigmoid(gate) * up).astype(BF16)


def _gate_up(x, g, w_gate_up, slot_src, tile_expert, tile_block, tile_subs, *, layer, tm, sub):
    d = x.shape[1]
    f = w_gate_up.shape[3] // 2
    nc = tm // sub
    fc = f // nc
    n_tiles = tile_expert.shape[0]
    cc = lambda t, c, ts: jnp.where(ts[t] > 0, c, nc - 1)
    body = functools.partial(_gate_up_body, tm=tm, sub=sub, n_tiles=n_tiles)
    grid_spec = pltpu.PrefetchScalarGridSpec(
        num_scalar_prefetch=4,
        grid=(n_tiles, nc),
        in_specs=[
            pl.BlockSpec(memory_space=pl.ANY),
            pl.BlockSpec((1, d), lambda t, c, te, tb, ts, src: (0, 0)),
            pl.BlockSpec((None, None, d, fc), lambda t, c, te, tb, ts, src: (layer, te[t], 0, cc(t, c, ts))),
            pl.BlockSpec((None, None, d, fc), lambda t, c, te, tb, ts, src: (layer, te[t], 0, nc + cc(t, c, ts))),
        ],
        out_specs=pl.BlockSpec((tm, fc), lambda t, c, te, tb, ts, src: (tb[t], cc(t, c, ts))),
        scratch_shapes=[pltpu.VMEM((tm, d), F32), pltpu.VMEM((tm, d), BF16), pltpu.SemaphoreType.DMA((1,))],
    )
    return pl.pallas_call(
        body,
        out_shape=jax.ShapeDtypeStruct((n_tiles * tm, f), BF16),
        grid_spec=grid_spec,
        compiler_params=_params("arbitrary", "arbitrary", gather=True),
        name="moe_gate_up",
    )(tile_expert, tile_block, tile_subs, slot_src, x, g, w_gate_up, w_gate_up)


def _down_body(te_ref, tb_ref, ts_ref, act_ref, wd_ref, o_ref, *, sub):
    n_here = ts_ref[pl.program_id(0)]
    for v in range(1, act_ref.shape[0] // sub + 1):
        @pl.when(n_here == v)
        def _():
            o_ref[:v * sub, :] = jnp.dot(act_ref[:v * sub, :], wd_ref[...].astype(BF16), preferred_element_type=F32)


def _down(act, w_down, tile_expert, tile_block, tile_subs, *, layer, tm, tn, sub):
    f, d = w_down.shape[2:]
    nj = d // tn
    n_tiles = tile_expert.shape[0]
    jc = lambda t, j, ts: jnp.where(ts[t] > 0, j, nj - 1)
    grid_spec = pltpu.PrefetchScalarGridSpec(
        num_scalar_prefetch=3,
        grid=(n_tiles, nj),
        in_specs=[
            pl.BlockSpec((tm, f), lambda t, j, te, tb, ts: (tb[t], 0)),
            pl.BlockSpec((None, None, f, tn), lambda t, j, te, tb, ts: (layer, te[t], 0, jc(t, j, ts))),
        ],
        out_specs=pl.BlockSpec((tm, tn), lambda t, j, te, tb, ts: (tb[t], jc(t, j, ts))),
    )
    return pl.pallas_call(
        functools.partial(_down_body, sub=sub),
        out_shape=jax.ShapeDtypeStruct((n_tiles * tm, d), F32),
        grid_spec=grid_spec,
        compiler_params=_params("arbitrary", "arbitrary"),
        name="moe_down",
    )(tile_expert, tile_block, tile_subs, act, w_down)


def _combine_body(ps_ref, x_ref, info_ref, ys_hbm, o_ref, gbuf, sem, *, tm, row_off, n_steps):
    i = pl.program_id(0)
    slot = i % 2

    def start_tile(step, buf):
        def issue(r, carry):
            for k in range(EXPERT_TOPK):
                s = ps_ref[(row_off + step * tm + r) * EXPERT_TOPK + k]
                pltpu.make_async_copy(ys_hbm.at[pl.ds(s, 1)], gbuf.at[buf, k, pl.ds(r, 1)], sem.at[buf]).start()
            return carry
        lax.fori_loop(0, tm, issue, 0, unroll=DMA_UNROLL // EXPERT_TOPK)

    @pl.when(i == 0)
    def _():
        start_tile(0, 0)

    @pl.when(i + 1 < n_steps)
    def _():
        start_tile(i + 1, 1 - slot)

    for k in range(EXPERT_TOPK):
        pltpu.make_async_copy(ys_hbm.at[pl.ds(0, tm)], gbuf.at[slot, k], sem.at[slot]).wait()
    w = info_ref[...]
    y = x_ref[...]
    for k in range(EXPERT_TOPK):
        y = y + w[:, EXPERT_TOPK + k:EXPERT_TOPK + k + 1] * gbuf[slot, k]
    o_ref[...] = y


def _combine(x, info, ys, pair_slot, *, tm, row_blk_off=0, n_rows=None):
    d = x.shape[1]
    n_rows = x.shape[0] if n_rows is None else n_rows
    n_steps = pl.cdiv(n_rows, tm)
    body = functools.partial(_combine_body, tm=tm, row_off=row_blk_off * tm, n_steps=n_steps)
    grid_spec = pltpu.PrefetchScalarGridSpec(
        num_scalar_prefetch=1,
        grid=(n_steps,),
        in_specs=[
            pl.BlockSpec((tm, d), lambda i, ps: (i + row_blk_off, 0)),
            pl.BlockSpec((tm, ROUTE_LANES), lambda i, ps: (i + row_blk_off, 0)),
            pl.BlockSpec(memory_space=pl.ANY),
        ],
        out_specs=pl.BlockSpec((tm, d), lambda i, ps: (i, 0)),
        scratch_shapes=[pltpu.VMEM((2, EXPERT_TOPK, tm, d), F32), pltpu.SemaphoreType.DMA((2,))],
    )
    return pl.pallas_call(
        body,
        out_shape=jax.ShapeDtypeStruct((n_rows, d), F32),
        grid_spec=grid_spec,
        compiler_params=_params("arbitrary", gather=True),
        name="moe_combine",
    )(pair_slot, x, info, ys)


def _moe_experts(x, g, w_rt_hi, w_rt_lo, b_rt, w_gate_up, w_down, *, layer, n_groups, per_group, tm_route, tm, sub,
                 tn, tm_c):
    m = x.shape[0]
    info = _route(x, g, w_rt_hi, w_rt_lo, b_rt, layer=layer, tm=tm_route, n_groups=n_groups, per_group=per_group)
    n_pad_pairs = pl.cdiv(m, tm_c) * tm_c * EXPERT_TOPK
    slot_src, pair_slot, te, tb, ts = _moe_plan(info, n_groups * per_group, tm, sub, n_pad_pairs)
    act = _gate_up(x, g, w_gate_up, slot_src, te, tb, ts, layer=layer, tm=tm, sub=sub)
    ys = _down(act, w_down, te, tb, ts, layer=layer, tm=tm, tn=tn, sub=sub)
    return info, ys, pair_slot


def _moba_prompt_body(slopes_ref, q_ref, gq_ref, k_ref, v_ref, o_ref, *, kv_group, blk):
    g, own = pl.program_id(1), pl.program_id(2)
    hd = k_ref.shape[1]
    nb = k_ref.shape[0] // blk
    cols = kv_group * blk
    q = q_ref[...]
    qn = jnp.concatenate([_rms(q[:, r * hd:(r + 1) * hd], gq_ref[...]) for r in range(kv_group)], axis=0)
    qb = qn.astype(BF16)
    qcol = lax.broadcasted_iota(jnp.int32, (1, cols), 1)
    slope = jnp.zeros((1, cols), F32)
    for r in range(kv_group):
        slope = jnp.where(qcol // blk == r, slopes_ref[g * kv_group + r], slope)
    tq = qcol % blk
    key = lax.broadcasted_iota(jnp.int32, (blk, 1), 0)
    rel = slope * (tq - key).astype(F32) * LOG2E
    scale = hd ** -0.5 * LOG2E

    nbp = -(-nb // 8) * 8
    km = jnp.concatenate([jnp.mean(k_ref[n * blk:(n + 1) * blk, :], axis=0, keepdims=True) for n in range(nb)]
                         + [jnp.zeros((1, hd), F32)] * (nbp - nb), axis=0)
    blk_id = lax.broadcasted_iota(jnp.int32, (nbp, cols), 0)
    gate = jnp.where(blk_id < own, _dot3(km, qn, _NT), NEG)
    cnt = jnp.zeros(gate.shape, jnp.int32)
    for m_ in range(nb):
        gm = gate[m_:m_ + 1, :]
        ahead = jnp.where(gm > gate, 1, jnp.where(gm == gate, jnp.where(m_ < blk_id, 1, 0), 0))
        cnt = cnt + ahead
    skip = jnp.where(cnt < MOBA_TOPK, jnp.where(blk_id < own, 0.0, -NEG), -NEG)
    pen = skip + slope * ((own - blk_id) * blk).astype(F32) * LOG2E

    def block_scores(n):
        start = pl.multiple_of(n * blk, blk)
        kb = k_ref[pl.ds(start, blk), :].astype(BF16)
        vt = v_ref[pl.ds(start, blk), :].T.astype(BF16)
        s = lax.dot_general(kb, qb, _NT, preferred_element_type=F32) * scale
        return s - rel, vt

    s, vt = block_scores(own)
    s = jnp.where(key <= tq, s, NEG)
    m0 = jnp.max(s, axis=0, keepdims=True)
    p = jnp.exp2(s - m0)
    l0 = jnp.sum(p, axis=0, keepdims=True)
    a0 = jnp.dot(vt, p.astype(BF16), preferred_element_type=F32)

    def step(n, carry):
        m_run, l_run, acc = carry
        s, vt = block_scores(n)
        s = s - jnp.sum(jnp.where(blk_id == n, pen, 0.0), axis=0, keepdims=True)
        m_new = jnp.maximum(m_run, jnp.max(s, axis=0, keepdims=True))
        alpha = jnp.exp2(m_run - m_new)
        p = jnp.exp2(s - m_new)
        l_new = alpha * l_run + jnp.sum(p, axis=0, keepdims=True)
        acc = alpha * acc + jnp.dot(vt, p.astype(BF16), preferred_element_type=F32)
        return m_new, l_new, acc

    _, l_fin, acc = lax.fori_loop(0, own, step, (m0, l0, a0))
    out = (acc / l_fin).T
    for r in range(kv_group):
        o_ref[:, r * hd:(r + 1) * hd] = out[r * blk:(r + 1) * blk]


def _moba_prompt(z, gq, kv, slopes, *, n_batch, t_len, n_kv, kv_group, out_rows):
    hd = gq.shape[1]
    blk = MOBA_BLOCK
    gw = kv_group * hd
    nt = t_len // blk
    body = functools.partial(_moba_prompt_body, kv_group=kv_group, blk=blk)
    grid_spec = pltpu.PrefetchScalarGridSpec(
        num_scalar_prefetch=1,
        grid=(n_batch, n_kv, nt),
        in_specs=[
            pl.BlockSpec((blk, gw), lambda b, g, t, s: (b * nt + t, g)),
            pl.BlockSpec((1, hd), lambda b, g, t, s: (0, 0)),
            pl.BlockSpec((t_len, hd), lambda b, g, t, s: (b, g)),
            pl.BlockSpec((t_len, hd), lambda b, g, t, s: (b, n_kv + g)),
        ],
        out_specs=pl.BlockSpec((blk, gw), lambda b, g, t, s: (b * nt + t, g)),
    )
    return pl.pallas_call(
        body,
        out_shape=jax.ShapeDtypeStruct((out_rows, n_kv * gw), F32),
        grid_spec=grid_spec,
        compiler_params=_params("parallel", "parallel", "arbitrary"),
        name="moba_prompt",
    )(slopes, z, gq, kv, kv)


def _sample_queries(q, gq, n_heads, hd):
    return jnp.concatenate([_rms(q[:, h * hd:(h + 1) * hd], gq) for h in range(n_heads)], axis=0)


def _row_slopes(slopes_ref, n_heads, t_len):
    row = lax.broadcasted_iota(jnp.int32, (n_heads * t_len, 1), 0)
    slope = jnp.zeros((n_heads * t_len, 1), F32)
    for h in range(n_heads):
        slope = jnp.where(row // t_len == h, slopes_ref[h], slope)
    return slope


def _moba_partial_body(pt_ref, slopes_ref, q_ref, gq_ref, k0_ref, k1_ref, v0_ref, v1_ref, po_ref, ps_ref,
                       qn_ref, bias_ref, *, n_kv, kv_group, past_len):
    n = pl.program_id(1)
    hd = gq_ref.shape[1]
    page = k0_ref.shape[0] // n_kv
    t_len = q_ref.shape[0]
    n_heads = n_kv * kv_group
    rows = n_heads * t_len
    grows = kv_group * t_len
    blk = 2 * page
    cols = blk * n_kv
    row = lax.broadcasted_iota(jnp.int32, (rows, 1), 0)

    @pl.when(n == 0)
    def _():
        qn_ref[...] = _sample_queries(q_ref[...], gq_ref[...], n_heads, hd)
        col = lax.broadcasted_iota(jnp.int32, (1, cols), 1)
        same_head = row // grows == col % n_kv
        slope = _row_slopes(slopes_ref, n_heads, t_len)
        bias_ref[...] = jnp.where(same_head, slope * (past_len + row % t_len - col // n_kv).astype(F32), -NEG) * LOG2E

    qn = qn_ref[...]
    kb = jnp.concatenate([k0_ref[...], k1_ref[...]], axis=0)
    vb = jnp.concatenate([v0_ref[...], v1_ref[...]], axis=0)
    s = lax.dot_general(qn.astype(BF16), kb.astype(BF16), _NT, preferred_element_type=F32) * (hd ** -0.5 * LOG2E)
    s = s - bias_ref[...]
    mx = jnp.max(s, axis=-1, keepdims=True)
    p = jnp.exp2(s - mx)
    den = jnp.sum(p, axis=-1, keepdims=True)
    po_ref[...] = jnp.dot(p.astype(BF16), vb.astype(BF16), preferred_element_type=F32)
    km = jnp.sum(kb.reshape(blk, n_kv, hd), axis=0) / blk
    km_rows = jnp.concatenate([jnp.broadcast_to(km[g:g + 1], (grows, hd)) for g in range(n_kv)], axis=0)
    gate = jnp.sum(qn * km_rows, axis=-1, keepdims=True)
    mx = mx / LOG2E + _row_slopes(slopes_ref, n_heads, t_len) * (n * blk).astype(F32)
    lane = lax.broadcasted_iota(jnp.int32, (rows, ROUTE_LANES), 1)
    ps_ref[...] = jnp.where(lane == 0, mx, jnp.where(lane == 1, den, jnp.where(lane == 2, gate, 0.0)))


def _moba_partials(z, gq, cache_k, cache_v, page_table, slopes, *, n_batch, t_len, row_off, kv_group, past_len):
    page, n_kv, hd = cache_k.shape[1:]
    pages_per_blk = MOBA_BLOCK // page
    assert pages_per_blk == 2
    n_pages = page_table.shape[1]
    nblk = n_pages // pages_per_blk
    n_heads = n_kv * kv_group
    rows = n_heads * t_len
    body = functools.partial(_moba_partial_body, n_kv=n_kv, kv_group=kv_group, past_len=past_len)
    cache_k = cache_k.reshape(cache_k.shape[0], page * n_kv, hd)
    cache_v = cache_v.reshape(cache_v.shape[0], page * n_kv, hd)
    page_spec = lambda which: pl.BlockSpec(
        (None, page * n_kv, hd), lambda b, n, pt, s: (pt[b * n_pages + n * pages_per_blk + which], 0, 0))
    grid_spec = pltpu.PrefetchScalarGridSpec(
        num_scalar_prefetch=2,
        grid=(n_batch, nblk),
        in_specs=[
            pl.BlockSpec((t_len, n_heads * hd), lambda b, n, pt, s: (row_off // t_len + b, 0)),
            pl.BlockSpec((1, hd), lambda b, n, pt, s: (0, 0)),
            page_spec(0), page_spec(1), page_spec(0), page_spec(1),
        ],
        out_specs=(
            pl.BlockSpec((None, None, rows, hd), lambda b, n, pt, s: (b, n, 0, 0)),
            pl.BlockSpec((None, None, rows, ROUTE_LANES), lambda b, n, pt, s: (b, n, 0, 0)),
        ),
        scratch_shapes=[pltpu.VMEM((rows, hd), F32), pltpu.VMEM((rows, MOBA_BLOCK * n_kv), F32)],
    )
    return pl.pallas_call(
        body,
        out_shape=(jax.ShapeDtypeStruct((n_batch, nblk, rows, hd), F32),
                   jax.ShapeDtypeStruct((n_batch, nblk, rows, ROUTE_LANES), F32)),
        grid_spec=grid_spec,
        compiler_params=_params("parallel", "arbitrary"),
        name="moba_partials",
    )(page_table.reshape(-1), slopes, z, gq, cache_k, cache_k, cache_v, cache_v)


def _moba_merge_body(slopes_ref, q_ref, gq_ref, kn_ref, vn_ref, po_ref, ps_ref, o_ref, *, n_kv, kv_group):
    hd = gq_ref.shape[1]
    t_len = q_ref.shape[0]
    n_heads = n_kv * kv_group
    rows = n_heads * t_len
    grows = kv_group * t_len
    nblk = ps_ref.shape[0]
    lane = lax.broadcasted_iota(jnp.int32, (rows, ROUTE_LANES), 1)
    m_blk = jnp.full((rows, ROUTE_LANES), NEG, F32)
    l_blk = jnp.zeros((rows, ROUTE_LANES), F32)
    gates = jnp.full((rows, ROUTE_LANES), NEG, F32)
    for n in range(nblk):
        st = ps_ref[n]
        m_blk = jnp.where(lane == n, st[:, 0:1], m_blk)
        l_blk = jnp.where(lane == n, st[:, 1:2], l_blk)
        gates = jnp.where(lane == n, st[:, 2:3], gates)
    w_blk = jnp.zeros((rows, ROUTE_LANES), F32)
    for _ in range(min(MOBA_TOPK, nblk)):
        best = jnp.max(gates, axis=-1, keepdims=True)
        first = jnp.min(jnp.where(gates == best, lane, ROUTE_LANES), axis=-1, keepdims=True)
        w_blk = jnp.where(lane == first, 1.0, w_blk)
        gates = jnp.where(lane == first, NEG, gates)
    m_blk = jnp.where(w_blk > 0.0, m_blk, NEG)
    m_past = jnp.max(m_blk, axis=-1, keepdims=True)

    qn = _sample_queries(q_ref[...], gq_ref[...], n_heads, hd)
    slope = _row_slopes(slopes_ref, n_heads, t_len)
    row = lax.broadcasted_iota(jnp.int32, (rows, 1), 0)
    tq = row % t_len
    tk = lax.broadcasted_iota(jnp.int32, (1, t_len), 1)
    s_own = jnp.concatenate([
        lax.dot_general(qn[g * grows:(g + 1) * grows].astype(BF16), kn_ref[:, g * hd:(g + 1) * hd].astype(BF16), _NT,
                        preferred_element_type=F32) for g in range(n_kv)], axis=0) * (hd ** -0.5)
    s_own = jnp.where(tk <= tq, s_own - slope * (tq - tk).astype(F32), NEG)
    m_all = jnp.maximum(m_past, jnp.max(s_own, axis=-1, keepdims=True))
    w_blk = jnp.where(w_blk > 0.0, jnp.exp(m_blk - m_all), 0.0)
    p_own = jnp.exp(s_own - m_all)
    den = jnp.sum(p_own, axis=-1, keepdims=True) + jnp.sum(w_blk * l_blk, axis=-1, keepdims=True)
    num = jnp.zeros((rows, hd), F32)
    for n in range(nblk):
        num = num + w_blk[:, n:n + 1] * po_ref[n]
    num = num + jnp.concatenate([
        jnp.dot(p_own[g * grows:(g + 1) * grows].astype(BF16), vn_ref[:, g * hd:(g + 1) * hd].astype(BF16),
                preferred_element_type=F32) for g in range(n_kv)], axis=0)
    out = num / den
    for h in range(n_heads):
        o_ref[:, h * hd:(h + 1) * hd] = out[h * t_len:(h + 1) * t_len]


def _moba_merge(z, gq, kv, part_o, part_s, slopes, *, n_batch, t_len, row_off, n_kv, kv_group, alias):
    hd = gq.shape[1]
    kw = n_kv * hd
    nblk, rows = part_o.shape[1:3]
    body = functools.partial(_moba_merge_body, n_kv=n_kv, kv_group=kv_group)
    wrapped = lambda *refs: body(*refs[:7], *refs[8:])
    new_rows = lambda b, s: (row_off // t_len + b, 0)
    grid_spec = pltpu.PrefetchScalarGridSpec(
        num_scalar_prefetch=1,
        grid=(n_batch,),
        in_specs=[
            pl.BlockSpec((t_len, kv_group * kw), new_rows),
            pl.BlockSpec((1, hd), lambda b, s: (0, 0)),
            pl.BlockSpec((t_len, kw), new_rows),
            pl.BlockSpec((t_len, kw), lambda b, s: (row_off // t_len + b, 1)),
            pl.BlockSpec((None, nblk, rows, hd), lambda b, s: (b, 0, 0, 0)),
            pl.BlockSpec((None, nblk, rows, ROUTE_LANES), lambda b, s: (b, 0, 0, 0)),
            pl.BlockSpec(memory_space=pl.ANY),
        ],
        out_specs=pl.BlockSpec((t_len, kv_group * kw), new_rows),
    )
    return pl.pallas_call(
        wrapped,
        out_shape=jax.ShapeDtypeStruct(alias.shape, F32),
        grid_spec=grid_spec,
        input_output_aliases={7: 0},
        compiler_params=_params("parallel"),
        name="moba_merge",
    )(slopes, z, gq, kv, kv, part_o, part_s, alias)


def _tile(m, pref):
    return pref if m >= pref else m


def kernel(x_prompt, x_sample, mem_prompt, cache_k, cache_v, cache_mem_k, cache_mem_v, state_pool, page_table, g_mix_norm, w_in, w_out, w_pool, pool_scale, g_q, g_kv_norm, w_kv, g_k, g_mem_norm, w_mem_kv, g_mem_q, g_mem_k, g_ffn_norm, w_group, b_group, w_router, b_router, w_gate_up, w_down):
    bp, tp, d = x_prompt.shape
    bs, ts, _ = x_sample.shape
    depth = w_in.shape[0]
    n_a = w_pool.shape[0]
    mem_tokens = mem_prompt.shape[1]
    mem_w = d // 4
    mem_hd = mem_w // MEM_HEADS
    tok_w = d - mem_w
    page, n_kv, hd = cache_k.shape[1:]
    kv_w = n_kv * hd
    n_heads = tok_w // hd
    kv_group = n_heads // n_kv
    past_len = page_table.shape[1] * page
    n_groups = w_group.shape[2]
    n_experts = w_router.shape[2]
    per_group = n_experts // n_groups
    d_expert = w_down.shape[2]
    mp, ms = bp * tp, bs * ts
    m_all = mp + ms

    tm_p = _tile(mp, 512)
    tm_d = _tile(mp, BIG_ROW_TILE)
    tn = _tile(d, 1024)
    tn_s = _tile(d, 512)
    tn_o = _tile(d, 512)
    tt_p = _tile(tp, MOBA_BLOCK)
    tn_e = _tile(d, 2048)
    tm_e = 1024 if m_all >= 4096 else 128
    sub_e = tm_e // 4
    tm_c = _tile(mp, 256)

    row = lambda v: v.reshape(1, -1).astype(F32)
    w_in_b, w_out_b, w_pool_b = w_in.astype(BF16), w_out.astype(BF16), w_pool.astype(BF16)
    w_kv_b, w_mem_kv_b = w_kv.astype(BF16)[None], w_mem_kv.astype(BF16)
    w_rt = jnp.concatenate([w_group, w_router, jnp.zeros((depth, d, ROUTE_LANES - n_groups - n_experts), F32)], axis=-1)
    w_rt_hi = w_rt.astype(BF16)
    w_rt_lo = (w_rt - w_rt_hi.astype(F32)).astype(BF16)
    b_rt = jnp.concatenate([b_group, b_router, jnp.zeros((depth, ROUTE_LANES - n_groups - n_experts), F32)],
                           axis=-1)[:, None, :]
    slopes = jnp.exp2(-8.0 * jnp.arange(1, n_heads + 1, dtype=F32) / n_heads)

    xp = x_prompt.reshape(mp, d)
    xs_ = x_sample.reshape(ms, d)
    sample_blk = mp // ms

    memx = mem_prompt.reshape(bp * mem_tokens, d)
    mem_kv_p = [
        _norm_matmul(memx, row(g_mem_norm[l]), w_mem_kv_b, layer=l, n_out=2 * mem_w, tm=_tile(bp * mem_tokens, 512),
                     tn=mem_w, head_groups=(mem_hd, 0),
                     gh=jnp.concatenate([jnp.tile(row(g_mem_k[l]), (1, MEM_HEADS)), jnp.ones((1, mem_w), F32)], axis=1))
        for l in range(depth)]
    mem_kv_p3 = [a.reshape(bp, mem_tokens, 2 * mem_w) for a in mem_kv_p]
    cmk = cache_mem_k.reshape(depth, bs, mem_tokens, mem_w)
    cmv = cache_mem_v.reshape(depth, bs, mem_tokens, mem_w)

    def mem_attend(z, l):
        y = _mem_attn(z, row(g_mem_q[l]),
                      mem_kv_p3[l], pl.BlockSpec((None, mem_tokens, mem_w), lambda b, t: (b, 0, 0)),
                      mem_kv_p3[l], pl.BlockSpec((None, mem_tokens, mem_w), lambda b, t: (b, 0, 1)),
                      n_batch=bp, t_len=tp, tt=tt_p, row_off=0, out_rows=m_all)
        return _mem_attn(z, row(g_mem_q[l]),
                         cmk, pl.BlockSpec((None, None, mem_tokens, mem_w), lambda b, t: (l, b, 0, 0)),
                         cmv, pl.BlockSpec((None, None, mem_tokens, mem_w), lambda b, t: (l, b, 0, 0)),
                         n_batch=bs, t_len=ts, tt=ts, row_off=mp, out_rows=m_all, alias=y)

    def moe_experts(x, l):
        return _moe_experts(x, row(g_ffn_norm[l]), w_rt_hi, w_rt_lo, b_rt, w_gate_up, w_down, layer=l,
                            n_groups=n_groups, per_group=per_group, tm_route=tm_p, tm=tm_e, sub=sub_e, tn=tn_e,
                            tm_c=tm_c)

    x = None
    z_pool = []
    kv = None
    y_prompt = y_sample = None
    for l in range(depth):
        g_mix = row(g_mix_norm[l])
        if l == 0:
            z = _norm_matmul(xp, g_mix, w_in_b, layer=l, n_out=d, tm=tm_d, tn=tn_o, out_rows=m_all)
            z = _norm_matmul(xs_, g_mix, w_in, layer=l, n_out=d, tm=ms, tn=tn_s, out_rows=m_all,
                             row_blk_off=sample_blk, alias=z)
        else:
            z = _norm_matmul(x, g_mix, w_in_b, layer=l, n_out=d, n_rows=mp, tm=tm_d, tn=tn_o, out_rows=m_all)
            z = _norm_matmul(x, g_mix, w_in_b, layer=l, n_out=d, n_rows=ms, tm=ms, tn=tn, in_blk_off=sample_blk,
                             out_rows=m_all, row_blk_off=sample_blk, alias=z)
        mem_y = mem_attend(z, l)
        if l < n_a:
            z_pool.append(z)
            scale = row(pool_scale[l])
            zero_buf = jnp.zeros((bp, POOL_HALO, tok_w), F32)
            samp_buf = jnp.concatenate([jnp.zeros((bs, 1, tok_w), F32), state_pool[l]], axis=1)
            tok_y = _pool_mix(z, zero_buf, w_pool_b[l], scale, n_batch=bp, t_len=tp, tt=tt_p, row_off=0, pos0=0,
                              out_rows=m_all)
            tok_y = _pool_mix(z, samp_buf, w_pool[l], scale, n_batch=bs, t_len=ts, tt=ts, row_off=mp,
                              pos0=past_len, out_rows=m_all, alias=tok_y)
        else:
            if kv is None:
                gh_kv = jnp.concatenate([jnp.tile(row(g_k), (1, n_kv)), jnp.ones((1, kv_w), F32)], axis=1)
                kv = _norm_matmul(x, row(g_kv_norm), w_kv_b, layer=0, n_out=2 * kv_w, n_rows=mp, tm=tm_d,
                                  tn=kv_w // 2, head_groups=(hd, hd, 0, 0), gh=gh_kv, out_rows=m_all)
                kv = _norm_matmul(x, row(g_kv_norm), w_kv_b, layer=0, n_out=2 * kv_w, n_rows=ms, tm=ms, tn=kv_w,
                                  head_groups=(hd, 0), gh=gh_kv, in_blk_off=sample_blk, out_rows=m_all,
                                  row_blk_off=sample_blk, alias=kv)
            gq = row(g_q[l - n_a])
            tok_y = _moba_prompt(z, gq, kv, slopes, n_batch=bp, t_len=tp, n_kv=n_kv, kv_group=kv_group,
                                 out_rows=m_all)
            part_o, part_s = _moba_partials(z, gq, cache_k, cache_v, page_table, slopes, n_batch=bs, t_len=ts,
                                            row_off=mp, kv_group=kv_group, past_len=past_len)
            tok_y = _moba_merge(z, gq, kv, part_o, part_s, slopes, n_batch=bs, t_len=ts, row_off=mp,
                                n_kv=n_kv, kv_group=kv_group, alias=tok_y)
        if l == 0:
            x = _out_proj(tok_y, mem_y, w_out_b, xp, layer=l, tm=tm_d, tn=tn_o, out_rows=m_all)
            x = _out_proj(tok_y, mem_y, w_out, xs_, layer=l, tm=ms, tn=tn_s, in_blk_off=sample_blk, out_rows=m_all,
                          alias=x)
        else:
            x_att = _out_proj(tok_y, mem_y, w_out_b, x, layer=l, n_rows=mp, tm=tm_d, tn=tn_o, out_rows=m_all)
            x = _out_proj(tok_y, mem_y, w_out_b, x, layer=l, n_rows=ms, tm=ms, tn=tn, in_blk_off=sample_blk,
                          x_blk_off=sample_blk, out_rows=m_all, alias=x_att)
        info, ys, pair_slot = moe_experts(x, l)
        if l + 1 < depth:
            x = _combine(x, info, ys, pair_slot, tm=tm_c)
        else:
            y_prompt = _combine(x, info, ys, pair_slot, tm=tm_c, n_rows=mp)
            y_sample = _combine(x, info, ys, pair_slot, tm=ms, row_blk_off=sample_blk, n_rows=ms)

    y_prompt = y_prompt.reshape(bp, tp, d)
    y_sample = y_sample.reshape(bs, ts, d)
    k_prompt = kv[:mp, :kv_w].reshape(bp, tp, n_kv, hd)
    v_prompt = kv[:mp, kv_w:].reshape(bp, tp, n_kv, hd)
    k_sample = kv[mp:, :kv_w].reshape(bs, ts, n_kv, hd)
    v_sample = kv[mp:, kv_w:].reshape(bs, ts, n_kv, hd)
    mem_k_prompt = jnp.stack([a[:, :, :mem_w].reshape(bp, mem_tokens, MEM_HEADS, mem_hd) for a in mem_kv_p3])
    mem_v_prompt = jnp.stack([a[:, :, mem_w:].reshape(bp, mem_tokens, MEM_HEADS, mem_hd) for a in mem_kv_p3])
    pool_prompt = jnp.stack([jnp.stack([zz[(b + 1) * tp - POOL_BUF:(b + 1) * tp, :tok_w] for b in range(bp)])
                             for zz in z_pool])
    pool_sample = jnp.stack([
        jnp.concatenate([state_pool[i], zz[mp:].reshape(bs, ts, d)[:, :, :tok_w]], axis=1)[:, -POOL_BUF:]
        for i, zz in enumerate(z_pool)])
    return (y_prompt, y_sample, k_prompt, v_prompt, k_sample, v_sample,
            mem_k_prompt, mem_v_prompt, pool_prompt, pool_sample)
```

```python
import functools

import jax
import jax.numpy as jnp
from jax import lax
from jax.experimental import pallas as pl
from jax.experimental.pallas import tpu as pltpu

F32 = jnp.float32
BF16 = jnp.bfloat16

EPS = 1e-6
POOL_WINDOWS = (2, 4, 8, 16)
POOL_BUF = max(POOL_WINDOWS) - 1
POOL_HALO = POOL_BUF + 1
MEM_HEADS = 4
MOBA_BLOCK = 256
MOBA_TOPK = 3
EXPERT_TOPK = 2
ROUTE_LANES = 128
NEG = -1e30
LOG2E = 1.4426950408889634
VMEM_LIMIT = 60 * 1024 * 1024
DMA_UNROLL = 8
BIG_ROW_TILE = 1024
NORM_ROWS = 256

_NT = (((1,), (1,)), ((), ()))
_NN = (((1,), (0,)), ((), ()))


def _params(*sem, gather=False):
    return pltpu.CompilerParams(dimension_semantics=sem, vmem_limit_bytes=VMEM_LIMIT, disable_bounds_checks=gather)


def _split_bf16(a):
    hi = a.astype(BF16)
    lo = (a - hi.astype(F32)).astype(BF16)
    return hi, lo


def _dot3(a, b, dims):
    ah, al = _split_bf16(a)
    bh, bl = _split_bf16(b)
    d = lambda x, y: lax.dot_general(x, y, dims, preferred_element_type=F32)
    return d(ah, bh) + (d(ah, bl) + d(al, bh))


def _rms(x, g):
    return x * lax.rsqrt(jnp.mean(x * x, axis=-1, keepdims=True) + EPS) * g


def _matmul(a, w):
    if w.dtype == F32:
        return _dot3(a.astype(F32), w, _NN)
    return jnp.dot(a.astype(BF16), w, preferred_element_type=F32)


def _with_alias(body, n_in, alias):
    if alias is None:
        return body, [], [], {}
    wrapped = lambda *refs: body(*refs[:n_in], *refs[n_in + 1:])
    return wrapped, [alias], [pl.BlockSpec(memory_space=pl.ANY)], {"input_output_aliases": {n_in: 0}}


def _norm_matmul_body(x_ref, g_ref, w_ref, gh_ref, o_ref, hn_ref, *, head_groups):
    j = pl.program_id(1)

    @pl.when(j == 0)
    def _():
        tm = x_ref.shape[0]
        step = min(tm, NORM_ROWS)
        for r0 in range(0, tm, step):
            hn_ref[r0:r0 + step, :] = _rms(x_ref[r0:r0 + step, :], g_ref[...]).astype(hn_ref.dtype)

    acc = _matmul(hn_ref[...], w_ref[...])
    tn = acc.shape[1]

    def write(gs):
        if gs == 0:
            o_ref[...] = acc
            return
        for c in range(tn // gs):
            sl = slice(c * gs, (c + 1) * gs)
            o_ref[:, sl] = _rms(acc[:, sl], gh_ref[:, sl])

    if len(set(head_groups)) == 1:
        write(head_groups[0])
    else:
        for jj, gs in enumerate(head_groups):
            pl.when(j == jj)(functools.partial(write, gs))


def _row_block_spec(shape, index_map, rows):
    if rows >= BIG_ROW_TILE:
        return pl.BlockSpec(shape, index_map, pipeline_mode=pl.Buffered(1))
    return pl.BlockSpec(shape, index_map)


def _norm_matmul(x, g, w, *, layer, n_out, tm, tn, head_groups=None, gh=None,
                 n_rows=None, in_blk_off=0, out_rows=None, row_blk_off=0, alias=None):
    k = x.shape[1]
    m = x.shape[0] if n_rows is None else n_rows
    nj = n_out // tn
    head_groups = tuple(head_groups) if head_groups else (0,) * nj
    if gh is None:
        gh = jnp.ones((1, n_out), F32)
    out_rows = m if out_rows is None else out_rows
    body = functools.partial(_norm_matmul_body, head_groups=head_groups)
    body, extra, extra_specs, kw = _with_alias(body, 4, alias)
    return pl.pallas_call(
        body,
        out_shape=jax.ShapeDtypeStruct((out_rows, n_out), F32),
        grid=(pl.cdiv(m, tm), nj),
        in_specs=[
            _row_block_spec((tm, k), lambda i, j: (i + in_blk_off, 0), tm),
            pl.BlockSpec((1, k), lambda i, j: (0, 0)),
            pl.BlockSpec((None, k, tn), lambda i, j: (layer, 0, j)),
            pl.BlockSpec((1, tn), lambda i, j: (0, j)),
        ] + extra_specs,
        out_specs=pl.BlockSpec((tm, tn), lambda i, j: (i + row_blk_off, j)),
        scratch_shapes=[pltpu.VMEM((tm, k), w.dtype)],
        compiler_params=_params("parallel", "arbitrary"),
        name="norm_matmul",
        **kw,
    )(x, g, w, gh, *extra)


def _pool_body(cur_ref, halo_ref, buf_ref, w_ref, scale_ref, o_ref, *, tt, pos0):
    t = pl.program_id(1)
    cur = cur_ref[...]
    halo = jnp.where(t == 0, buf_ref[...], halo_ref[...])
    full = jnp.concatenate([halo, cur], axis=0)
    pos = pos0 + t * tt + lax.broadcasted_iota(jnp.int32, (tt, 1), 0)
    gw = cur.shape[1] // len(POOL_WINDOWS)
    for gi, win in enumerate(POOL_WINDOWS):
        sl = slice(gi * gw, (gi + 1) * gw)
        s = full[:, sl]
        sh = 1
        while sh < win:
            s = s + pltpu.roll(s, shift=sh, axis=0)
            sh *= 2
        cnt = jnp.minimum(pos + 1, win).astype(F32)
        d = s[POOL_HALO:] / cnt - cur[:, sl]
        y = _matmul(d, w_ref[gi])
        o_ref[:, sl] = y * scale_ref[:, sl]


def _pool_mix(z, buf16, w_pool, scale, *, n_batch, t_len, tt, row_off, pos0, out_rows, alias=None):
    tok_w = scale.shape[1]
    nt = t_len // tt
    cur_blk = lambda b, t: ((row_off + b * t_len) // tt + t, 0)
    halo_blk = lambda b, t: (jnp.maximum((row_off + b * t_len + t * tt) // POOL_HALO - 1, 0), 0)
    body = functools.partial(_pool_body, tt=tt, pos0=pos0)
    body, extra, extra_specs, kw = _with_alias(body, 5, alias)
    return pl.pallas_call(
        body,
        out_shape=jax.ShapeDtypeStruct((out_rows, tok_w), F32),
        grid=(n_batch, nt),
        in_specs=[
            pl.BlockSpec((tt, tok_w), cur_blk),
            pl.BlockSpec((POOL_HALO, tok_w), halo_blk),
            pl.BlockSpec((None, POOL_HALO, tok_w), lambda b, t: (b, 0, 0)),
            pl.BlockSpec(w_pool.shape, lambda b, t: (0, 0, 0)),
            pl.BlockSpec((1, tok_w), lambda b, t: (0, 0)),
        ] + extra_specs,
        out_specs=pl.BlockSpec((tt, tok_w), cur_blk),
        compiler_params=_params("parallel", "arbitrary"),
        name="pool_mix",
        **kw,
    )(z, z, buf16, w_pool, scale, *extra)


def _mem_attn_body(q_ref, g_ref, k_ref, v_ref, o_ref):
    q = q_ref[...]
    hd = q.shape[1] // MEM_HEADS
    for h in range(MEM_HEADS):
        sl = slice(h * hd, (h + 1) * hd)
        qh = _rms(q[:, sl], g_ref[...]).astype(BF16)
        s = lax.dot_general(qh, k_ref[:, sl].astype(BF16), _NT, preferred_element_type=F32) * (hd ** -0.5)
        e = jnp.exp(s - jnp.max(s, axis=-1, keepdims=True))
        p = (e / jnp.sum(e, axis=-1, keepdims=True)).astype(BF16)
        o_ref[:, sl] = jnp.dot(p, v_ref[:, sl].astype(BF16), preferred_element_type=F32)


def _mem_attn(z, g, k_arr, k_spec, v_arr, v_spec, *, n_batch, t_len, tt, row_off, out_rows, alias=None):
    mem_w = g.shape[1] * MEM_HEADS
    col_blk = z.shape[1] // mem_w - 1
    nt = t_len // tt
    body, extra, extra_specs, kw = _with_alias(_mem_attn_body, 4, alias)
    return pl.pallas_call(
        body,
        out_shape=jax.ShapeDtypeStruct((out_rows, mem_w), F32),
        grid=(n_batch, nt),
        in_specs=[
            pl.BlockSpec((tt, mem_w), lambda b, t: ((row_off + b * t_len) // tt + t, col_blk)),
            pl.BlockSpec((1, g.shape[1]), lambda b, t: (0, 0)),
            k_spec,
            v_spec,
        ] + extra_specs,
        out_specs=pl.BlockSpec((tt, mem_w), lambda b, t: ((row_off + b * t_len) // tt + t, 0)),
        compiler_params=_params("parallel", "arbitrary"),
        name="mem_attn",
        **kw,
    )(z, g, k_arr, v_arr, *extra)


def _out_proj_body(tok_ref, mem_ref, w_ref, x_ref, o_ref, cat_ref):
    tok_w = tok_ref.shape[1]

    @pl.when(pl.program_id(1) == 0)
    def _():
        cat_ref[:, :tok_w] = tok_ref[...].astype(cat_ref.dtype)
        cat_ref[:, tok_w:] = mem_ref[...].astype(cat_ref.dtype)

    o_ref[...] = x_ref[...] + _matmul(cat_ref[...], w_ref[...])


def _out_proj(tok, mem, w, x, *, layer, tm, tn, n_rows=None, in_blk_off=0, x_blk_off=0, out_rows=None, alias=None):
    d = x.shape[1]
    m = x.shape[0] if n_rows is None else n_rows
    tok_w, mem_w = tok.shape[1], mem.shape[1]
    out_rows = m if out_rows is None else out_rows
    body, extra, extra_specs, kw = _with_alias(_out_proj_body, 4, alias)
    return pl.pallas_call(
        body,
        out_shape=jax.ShapeDtypeStruct((out_rows, d), F32),
        grid=(pl.cdiv(m, tm), d // tn),
        in_specs=[
            _row_block_spec((tm, tok_w), lambda i, j: (i + in_blk_off, 0), tm),
            _row_block_spec((tm, mem_w), lambda i, j: (i + in_blk_off, 0), tm),
            pl.BlockSpec((None, tok_w + mem_w, tn), lambda i, j: (layer, 0, j)),
            pl.BlockSpec((tm, tn), lambda i, j: (i + x_blk_off, j)),
        ] + extra_specs,
        out_specs=pl.BlockSpec((tm, tn), lambda i, j: (i + in_blk_off, j)),
        scratch_shapes=[pltpu.VMEM((tm, tok_w + mem_w), w.dtype)],
        compiler_params=_params("parallel", "arbitrary"),
        name="out_proj",
        **kw,
    )(tok, mem, w, x, *extra)


def _route_body(x_ref, g_ref, wh_ref, wl_ref, b_ref, info_ref, *, n_groups, per_group):
    h = _rms(x_ref[...], g_ref[...])
    hh, hl = _split_bf16(h)
    d = lambda a, b: jnp.dot(a, b, preferred_element_type=F32)
    logits = d(hh, wh_ref[...]) + (d(hh, wl_ref[...]) + d(hl, wh_ref[...])) + b_ref[...]
    lane = lax.broadcasted_iota(jnp.int32, logits.shape, 1)
    big = jnp.int32(ROUTE_LANES)

    def first_max(mask):
        v = jnp.max(jnp.where(mask, logits, NEG), axis=-1, keepdims=True)
        i = jnp.min(jnp.where(mask & (logits == v), lane, big), axis=-1, keepdims=True)
        return v, i

    g_mask = lane < n_groups
    g_max, g_sel = first_max(g_mask)
    p_group = 1.0 / jnp.sum(jnp.where(g_mask, jnp.exp(logits - g_max), 0.0), axis=-1, keepdims=True)
    lo = n_groups + g_sel * per_group
    e_mask = (lane >= lo) & (lane < lo + per_group)
    v1, i1 = first_max(e_mask)
    v2, i2 = first_max(e_mask & (lane != i1))
    e2 = jnp.exp(v2 - v1)
    w1 = p_group / (1.0 + e2)
    w2 = p_group * e2 / (1.0 + e2)
    id1 = (i1 - n_groups).astype(F32)
    id2 = (i2 - n_groups).astype(F32)
    info_ref[...] = jnp.where(lane == 0, id1, jnp.where(lane == 1, id2,
                              jnp.where(lane == 2, w1, jnp.where(lane == 3, w2, 0.0))))


def _route(x, g, w_hi, w_lo, bias, *, layer, tm, n_groups, per_group):
    m, d = x.shape
    body = functools.partial(_route_body, n_groups=n_groups, per_group=per_group)
    return pl.pallas_call(
        body,
        out_shape=jax.ShapeDtypeStruct((m, ROUTE_LANES), F32),
        grid=(pl.cdiv(m, tm),),
        in_specs=[
            pl.BlockSpec((tm, d), lambda i: (i, 0)),
            pl.BlockSpec((1, d), lambda i: (0, 0)),
            pl.BlockSpec((None, d, ROUTE_LANES), lambda i: (layer, 0, 0)),
            pl.BlockSpec((None, d, ROUTE_LANES), lambda i: (layer, 0, 0)),
            pl.BlockSpec((None, 1, ROUTE_LANES), lambda i: (layer, 0, 0)),
        ],
        out_specs=pl.BlockSpec((tm, ROUTE_LANES), lambda i: (i, 0)),
        compiler_params=_params("parallel"),
        name="moe_route",
    )(x, g, w_hi, w_lo, bias)


def _moe_plan(info, n_experts, tm, sub, n_pad_pairs):
    m = info.shape[0]
    ids = info[:, :EXPERT_TOPK].astype(jnp.int32).reshape(-1)
    n_pairs = m * EXPERT_TOPK
    n_tiles = n_pairs // tm + n_experts
    n_slots = n_tiles * tm
    onehot = (ids[:, None] == jnp.arange(n_experts)[None, :]).astype(jnp.int32)
    csum = jnp.cumsum(onehot, axis=0)
    rank = jnp.sum(csum * onehot, axis=1) - 1
    counts = csum[-1]
    tiles_e = (counts + tm - 1) // tm
    tile_end = jnp.cumsum(tiles_e)
    tile_start = tile_end - tiles_e
    slot_of_pair = jnp.sum(onehot * tile_start[None, :], axis=1) * tm + rank
    total = tile_end[-1]
    t_idx = jnp.arange(n_tiles, dtype=jnp.int32)
    t_eff = jnp.minimum(t_idx, total - 1).astype(jnp.int32)
    tile_expert = jnp.sum((tile_end[None, :] <= t_eff[:, None]).astype(jnp.int32), axis=1)
    te_hot = (tile_expert[:, None] == jnp.arange(n_experts)[None, :]).astype(jnp.int32)
    rows_left = jnp.sum(te_hot * (counts[None, :] - (t_eff[:, None] - tile_start[None, :]) * tm), axis=1)
    tile_subs = jnp.where(t_idx < total, (jnp.clip(rows_left, 0, tm) + sub - 1) // sub, 0).astype(jnp.int32)
    pair = jnp.arange(n_pairs, dtype=jnp.int32)
    slot_src = jnp.zeros((n_slots,), jnp.int32).at[slot_of_pair].set(pair // EXPERT_TOPK, unique_indices=True)
    pair_slot = jnp.concatenate([slot_of_pair.astype(jnp.int32), jnp.zeros((n_pad_pairs - n_pairs,), jnp.int32)])
    return slot_src, pair_slot, tile_expert.astype(jnp.int32), t_eff, tile_subs


def _gate_up_body(te_ref, tb_ref, ts_ref, src_ref, x_hbm, g_ref, wg_ref, wu_ref, act_ref, xbuf, xn_ref, sem,
                  *, tm, sub, n_tiles):
    t, c = pl.program_id(0), pl.program_id(1)
    nsub = tm // sub
    n_here = ts_ref[t]

    def start_sub(tile, sb):
        def issue(i, carry):
            r = sb * sub + i
            tok = src_ref[tile * tm + r]
            pltpu.make_async_copy(x_hbm.at[pl.ds(tok, 1)], xbuf.at[pl.ds(r, 1)], sem.at[0]).start()
            return carry
        lax.fori_loop(0, sub, issue, 0, unroll=DMA_UNROLL)

    def wait_tile():
        for _ in range(nsub):
            pltpu.make_async_copy(x_hbm.at[pl.ds(0, sub)], xbuf.at[pl.ds(0, sub)], sem.at[0]).wait()

    @pl.when((t == 0) & (c == 0))
    def _():
        for sb in range(nsub):
            start_sub(0, sb)

    @pl.when((c == 0) & (n_here > 0))
    def _():
        wait_tile()
        for sb in range(nsub):
            @pl.when(sb < n_here)
            def _():
                rows = slice(sb * sub, (sb + 1) * sub)
                xn_ref[rows, :] = _rms(xbuf[rows, :], g_ref[...]).astype(BF16)

    @pl.when((c == 0) & (n_here == 0) & (t > 0) & (ts_ref[jnp.maximum(t - 1, 0)] > 0))
    def _():
        wait_tile()

    nxt = jnp.minimum(t + 1, n_tiles - 1)

    for v in range(1, nsub + 1):
        @pl.when(n_here == v)
        def _():
            for i in range(sub):
                r = c * sub + i
                tok = src_ref[nxt * tm + r]
                pltpu.make_async_copy(x_hbm.at[pl.ds(tok, 1)], xbuf.at[pl.ds(r, 1)], sem.at[0]).start()
            x = xn_ref[:v * sub, :]
            gate = jnp.dot(x, wg_ref[...].astype(BF16), preferred_element_type=F32)
            up = jnp.dot(x, wu_ref[...].astype(BF16), preferred_element_type=F32)
            act_ref[:v * sub, :] = (gate * jax.nn.sigmoid(gate) * up).astype(BF16)

    @pl.when((t == n_tiles - 1) & (c == nsub - 1) & (n_here > 0))
    def _():
        wait_tile()


def _gate_up(x, g, w_gate_up, slot_src, tile_expert, tile_block, tile_subs, *, layer, tm, sub):
    d = x.shape[1]
    f = w_gate_up.shape[3] // 2
    nc = tm // sub
    fc = f // nc
    n_tiles = tile_expert.shape[0]
    cc = lambda t, c, ts: jnp.where(ts[t] > 0, c, nc - 1)
    body = functools.partial(_gate_up_body, tm=tm, sub=sub, n_tiles=n_tiles)
    grid_spec = pltpu.PrefetchScalarGridSpec(
        num_scalar_prefetch=4,
        grid=(n_tiles, nc),
        in_specs=[
            pl.BlockSpec(memory_space=pl.ANY),
            pl.BlockSpec((1, d), lambda t, c, te, tb, ts, src: (0, 0)),
            pl.BlockSpec((None, None, d, fc), lambda t, c, te, tb, ts, src: (layer, te[t], 0, cc(t, c, ts))),
            pl.BlockSpec((None, None, d, fc), lambda t, c, te, tb, ts, src: (layer, te[t], 0, nc + cc(t, c, ts))),
        ],
        out_specs=pl.BlockSpec((tm, fc), lambda t, c, te, tb, ts, src: (tb[t], cc(t, c, ts))),
        scratch_shapes=[pltpu.VMEM((tm, d), F32), pltpu.VMEM((tm, d), BF16), pltpu.SemaphoreType.DMA((1,))],
    )
    return pl.pallas_call(
        body,
        out_shape=jax.ShapeDtypeStruct((n_tiles * tm, f), BF16),
        grid_spec=grid_spec,
        compiler_params=_params("arbitrary", "arbitrary", gather=True),
        name="moe_gate_up",
    )(tile_expert, tile_block, tile_subs, slot_src, x, g, w_gate_up, w_gate_up)


def _down_body(te_ref, tb_ref, ts_ref, act_ref, wd_ref, o_ref, *, sub):
    n_here = ts_ref[pl.program_id(0)]
    for v in range(1, act_ref.shape[0] // sub + 1):
        @pl.when(n_here == v)
        def _():
            o_ref[:v * sub, :] = jnp.dot(act_ref[:v * sub, :], wd_ref[...].astype(BF16), preferred_element_type=F32)


def _down(act, w_down, tile_expert, tile_block, tile_subs, *, layer, tm, tn, sub):
    f, d = w_down.shape[2:]
    nj = d // tn
    n_tiles = tile_expert.shape[0]
    jc = lambda t, j, ts: jnp.where(ts[t] > 0, j, nj - 1)
    grid_spec = pltpu.PrefetchScalarGridSpec(
        num_scalar_prefetch=3,
        grid=(n_tiles, nj),
        in_specs=[
            pl.BlockSpec((tm, f), lambda t, j, te, tb, ts: (tb[t], 0)),
            pl.BlockSpec((None, None, f, tn), lambda t, j, te, tb, ts: (layer, te[t], 0, jc(t, j, ts))),
        ],
        out_specs=pl.BlockSpec((tm, tn), lambda t, j, te, tb, ts: (tb[t], jc(t, j, ts))),
    )
    return pl.pallas_call(
        functools.partial(_down_body, sub=sub),
        out_shape=jax.ShapeDtypeStruct((n_tiles * tm, d), F32),
        grid_spec=grid_spec,
        compiler_params=_params("arbitrary", "arbitrary"),
        name="moe_down",
    )(tile_expert, tile_block, tile_subs, act, w_down)


def _combine_body(ps_ref, x_ref, info_ref, ys_hbm, o_ref, gbuf, sem, *, tm, row_off, n_steps):
    i = pl.program_id(0)
    slot = i % 2

    def start_tile(step, buf):
        def issue(r, carry):
            for k in range(EXPERT_TOPK):
                s = ps_ref[(row_off + step * tm + r) * EXPERT_TOPK + k]
                pltpu.make_async_copy(ys_hbm.at[pl.ds(s, 1)], gbuf.at[buf, k, pl.ds(r, 1)], sem.at[buf]).start()
            return carry
        lax.fori_loop(0, tm, issue, 0, unroll=DMA_UNROLL // EXPERT_TOPK)

    @pl.when(i == 0)
    def _():
        start_tile(0, 0)

    @pl.when(i + 1 < n_steps)
    def _():
        start_tile(i + 1, 1 - slot)

    for k in range(EXPERT_TOPK):
        pltpu.make_async_copy(ys_hbm.at[pl.ds(0, tm)], gbuf.at[slot, k], sem.at[slot]).wait()
    w = info_ref[...]
    y = x_ref[...]
    for k in range(EXPERT_TOPK):
        y = y + w[:, EXPERT_TOPK + k:EXPERT_TOPK + k + 1] * gbuf[slot, k]
    o_ref[...] = y


def _combine(x, info, ys, pair_slot, *, tm, row_blk_off=0, n_rows=None):
    d = x.shape[1]
    n_rows = x.shape[0] if n_rows is None else n_rows
    n_steps = pl.cdiv(n_rows, tm)
    body = functools.partial(_combine_body, tm=tm, row_off=row_blk_off * tm, n_steps=n_steps)
    grid_spec = pltpu.PrefetchScalarGridSpec(
        num_scalar_prefetch=1,
        grid=(n_steps,),
        in_specs=[
            pl.BlockSpec((tm, d), lambda i, ps: (i + row_blk_off, 0)),
            pl.BlockSpec((tm, ROUTE_LANES), lambda i, ps: (i + row_blk_off, 0)),
            pl.BlockSpec(memory_space=pl.ANY),
        ],
        out_specs=pl.BlockSpec((tm, d), lambda i, ps: (i, 0)),
        scratch_shapes=[pltpu.VMEM((2, EXPERT_TOPK, tm, d), F32), pltpu.SemaphoreType.DMA((2,))],
    )
    return pl.pallas_call(
        body,
        out_shape=jax.ShapeDtypeStruct((n_rows, d), F32),
        grid_spec=grid_spec,
        compiler_params=_params("arbitrary", gather=True),
        name="moe_combine",
    )(pair_slot, x, info, ys)


def _moe_experts(x, g, w_rt_hi, w_rt_lo, b_rt, w_gate_up, w_down, *, layer, n_groups, per_group, tm_route, tm, sub,
                 tn, tm_c):
    m = x.shape[0]
    info = _route(x, g, w_rt_hi, w_rt_lo, b_rt, layer=layer, tm=tm_route, n_groups=n_groups, per_group=per_group)
    n_pad_pairs = pl.cdiv(m, tm_c) * tm_c * EXPERT_TOPK
    slot_src, pair_slot, te, tb, ts = _moe_plan(info, n_groups * per_group, tm, sub, n_pad_pairs)
    act = _gate_up(x, g, w_gate_up, slot_src, te, tb, ts, layer=layer, tm=tm, sub=sub)
    ys = _down(act, w_down, te, tb, ts, layer=layer, tm=tm, tn=tn, sub=sub)
    return info, ys, pair_slot


def _moba_prompt_body(slopes_ref, q_ref, gq_ref, k_ref, v_ref, o_ref, *, kv_group, blk):
    g, own = pl.program_id(1), pl.program_id(2)
    hd = k_ref.shape[1]
    nb = k_ref.shape[0] // blk
    cols = kv_group * blk
    q = q_ref[...]
    qn = jnp.concatenate([_rms(q[:, r * hd:(r + 1) * hd], gq_ref[...]) for r in range(kv_group)], axis=0)
    qb = qn.astype(BF16)
    qcol = lax.broadcasted_iota(jnp.int32, (1, cols), 1)
    slope = jnp.zeros((1, cols), F32)
    for r in range(kv_group):
        slope = jnp.where(qcol // blk == r, slopes_ref[g * kv_group + r], slope)
    tq = qcol % blk
    key = lax.broadcasted_iota(jnp.int32, (blk, 1), 0)
    rel = slope * (tq - key).astype(F32) * LOG2E
    scale = hd ** -0.5 * LOG2E

    nbp = -(-nb // 8) * 8
    km = jnp.concatenate([jnp.mean(k_ref[n * blk:(n + 1) * blk, :], axis=0, keepdims=True) for n in range(nb)]
                         + [jnp.zeros((1, hd), F32)] * (nbp - nb), axis=0)
    blk_id = lax.broadcasted_iota(jnp.int32, (nbp, cols), 0)
    gate = jnp.where(blk_id < own, _dot3(km, qn, _NT), NEG)
    cnt = jnp.zeros(gate.shape, jnp.int32)
    for m_ in range(nb):
        gm = gate[m_:m_ + 1, :]
        ahead = jnp.where(gm > gate, 1, jnp.where(gm == gate, jnp.where(m_ < blk_id, 1, 0), 0))
        cnt = cnt + ahead
    skip = jnp.where(cnt < MOBA_TOPK, jnp.where(blk_id < own, 0.0, -NEG), -NEG)
    pen = skip + slope * ((own - blk_id) * blk).astype(F32) * LOG2E

    def block_scores(n):
        start = pl.multiple_of(n * blk, blk)
        kb = k_ref[pl.ds(start, blk), :].astype(BF16)
        vt = v_ref[pl.ds(start, blk), :].T.astype(BF16)
        s = lax.dot_general(kb, qb, _NT, preferred_element_type=F32) * scale
        return s - rel, vt

    s, vt = block_scores(own)
    s = jnp.where(key <= tq, s, NEG)
    m0 = jnp.max(s, axis=0, keepdims=True)
    p = jnp.exp2(s - m0)
    l0 = jnp.sum(p, axis=0, keepdims=True)
    a0 = jnp.dot(vt, p.astype(BF16), preferred_element_type=F32)

    def step(n, carry):
        m_run, l_run, acc = carry
        s, vt = block_scores(n)
        s = s - jnp.sum(jnp.where(blk_id == n, pen, 0.0), axis=0, keepdims=True)
        m_new = jnp.maximum(m_run, jnp.max(s, axis=0, keepdims=True))
        alpha = jnp.exp2(m_run - m_new)
        p = jnp.exp2(s - m_new)
        l_new = alpha * l_run + jnp.sum(p, axis=0, keepdims=True)
        acc = alpha * acc + jnp.dot(vt, p.astype(BF16), preferred_element_type=F32)
        return m_new, l_new, acc

    _, l_fin, acc = lax.fori_loop(0, own, step, (m0, l0, a0))
    out = (acc / l_fin).T
    for r in range(kv_group):
        o_ref[:, r * hd:(r + 1) * hd] = out[r * blk:(r + 1) * blk]


def _moba_prompt(z, gq, kv, slopes, *, n_batch, t_len, n_kv, kv_group, out_rows):
    hd = gq.shape[1]
    blk = MOBA_BLOCK
    gw = kv_group * hd
    nt = t_len // blk
    body = functools.partial(_moba_prompt_body, kv_group=kv_group, blk=blk)
    grid_spec = pltpu.PrefetchScalarGridSpec(
        num_scalar_prefetch=1,
        grid=(n_batch, n_kv, nt),
        in_specs=[
            pl.BlockSpec((blk, gw), lambda b, g, t, s: (b * nt + t, g)),
            pl.BlockSpec((1, hd), lambda b, g, t, s: (0, 0)),
            pl.BlockSpec((t_len, hd), lambda b, g, t, s: (b, g)),
            pl.BlockSpec((t_len, hd), lambda b, g, t, s: (b, n_kv + g)),
        ],
        out_specs=pl.BlockSpec((blk, gw), lambda b, g, t, s: (b * nt + t, g)),
    )
    return pl.pallas_call(
        body,
        out_shape=jax.ShapeDtypeStruct((out_rows, n_kv * gw), F32),
        grid_spec=grid_spec,
        compiler_params=_params("parallel", "parallel", "arbitrary"),
        name="moba_prompt",
    )(slopes, z, gq, kv, kv)


def _sample_queries(q, gq, n_heads, hd):
    return jnp.concatenate([_rms(q[:, h * hd:(h + 1) * hd], gq) for h in range(n_heads)], axis=0)


def _row_slopes(slopes_ref, n_heads, t_len):
    row = lax.broadcasted_iota(jnp.int32, (n_heads * t_len, 1), 0)
    slope = jnp.zeros((n_heads * t_len, 1), F32)
    for h in range(n_heads):
        slope = jnp.where(row // t_len == h, slopes_ref[h], slope)
    return slope


def _moba_partial_body(pt_ref, slopes_ref, q_ref, gq_ref, k0_ref, k1_ref, v0_ref, v1_ref, po_ref, ps_ref,
                       qn_ref, bias_ref, *, n_kv, kv_group, past_len):
    n = pl.program_id(1)
    hd = gq_ref.shape[1]
    page = k0_ref.shape[0] // n_kv
    t_len = q_ref.shape[0]
    n_heads = n_kv * kv_group
    rows = n_heads * t_len
    grows = kv_group * t_len
    blk = 2 * page
    cols = blk * n_kv
    row = lax.broadcasted_iota(jnp.int32, (rows, 1), 0)

    @pl.when(n == 0)
    def _():
        qn_ref[...] = _sample_queries(q_ref[...], gq_ref[...], n_heads, hd)
        col = lax.broadcasted_iota(jnp.int32, (1, cols), 1)
        same_head = row // grows == col % n_kv
        slope = _row_slopes(slopes_ref, n_heads, t_len)
        bias_ref[...] = jnp.where(same_head, slope * (past_len + row % t_len - col // n_kv).astype(F32), -NEG) * LOG2E

    qn = qn_ref[...]
    kb = jnp.concatenate([k0_ref[...], k1_ref[...]], axis=0)
    vb = jnp.concatenate([v0_ref[...], v1_ref[...]], axis=0)
    s = lax.dot_general(qn.astype(BF16), kb.astype(BF16), _NT, preferred_element_type=F32) * (hd ** -0.5 * LOG2E)
    s = s - bias_ref[...]
    mx = jnp.max(s, axis=-1, keepdims=True)
    p = jnp.exp2(s - mx)
    den = jnp.sum(p, axis=-1, keepdims=True)
    po_ref[...] = jnp.dot(p.astype(BF16), vb.astype(BF16), preferred_element_type=F32)
    km = jnp.sum(kb.reshape(blk, n_kv, hd), axis=0) / blk
    km_rows = jnp.concatenate([jnp.broadcast_to(km[g:g + 1], (grows, hd)) for g in range(n_kv)], axis=0)
    gate = jnp.sum(qn * km_rows, axis=-1, keepdims=True)
    mx = mx / LOG2E + _row_slopes(slopes_ref, n_heads, t_len) * (n * blk).astype(F32)
    lane = lax.broadcasted_iota(jnp.int32, (rows, ROUTE_LANES), 1)
    ps_ref[...] = jnp.where(lane == 0, mx, jnp.where(lane == 1, den, jnp.where(lane == 2, gate, 0.0)))


def _moba_partials(z, gq, cache_k, cache_v, page_table, slopes, *, n_batch, t_len, row_off, kv_group, past_len):
    page, n_kv, hd = cache_k.shape[1:]
    pages_per_blk = MOBA_BLOCK // page
    assert pages_per_blk == 2
    n_pages = page_table.shape[1]
    nblk = n_pages // pages_per_blk
    n_heads = n_kv * kv_group
    rows = n_heads * t_len
    body = functools.partial(_moba_partial_body, n_kv=n_kv, kv_group=kv_group, past_len=past_len)
    cache_k = cache_k.reshape(cache_k.shape[0], page * n_kv, hd)
    cache_v = cache_v.reshape(cache_v.shape[0], page * n_kv, hd)
    page_spec = lambda which: pl.BlockSpec(
        (None, page * n_kv, hd), lambda b, n, pt, s: (pt[b * n_pages + n * pages_per_blk + which], 0, 0))
    grid_spec = pltpu.PrefetchScalarGridSpec(
        num_scalar_prefetch=2,
        grid=(n_batch, nblk),
        in_specs=[
            pl.BlockSpec((t_len, n_heads * hd), lambda b, n, pt, s: (row_off // t_len + b, 0)),
            pl.BlockSpec((1, hd), lambda b, n, pt, s: (0, 0)),
            page_spec(0), page_spec(1), page_spec(0), page_spec(1),
        ],
        out_specs=(
            pl.BlockSpec((None, None, rows, hd), lambda b, n, pt, s: (b, n, 0, 0)),
            pl.BlockSpec((None, None, rows, ROUTE_LANES), lambda b, n, pt, s: (b, n, 0, 0)),
        ),
        scratch_shapes=[pltpu.VMEM((rows, hd), F32), pltpu.VMEM((rows, MOBA_BLOCK * n_kv), F32)],
    )
    return pl.pallas_call(
        body,
        out_shape=(jax.ShapeDtypeStruct((n_batch, nblk, rows, hd), F32),
                   jax.ShapeDtypeStruct((n_batch, nblk, rows, ROUTE_LANES), F32)),
        grid_spec=grid_spec,
        compiler_params=_params("parallel", "arbitrary"),
        name="moba_partials",
    )(page_table.reshape(-1), slopes, z, gq, cache_k, cache_k, cache_v, cache_v)


def _moba_merge_body(slopes_ref, q_ref, gq_ref, kn_ref, vn_ref, po_ref, ps_ref, o_ref, *, n_kv, kv_group):
    hd = gq_ref.shape[1]
    t_len = q_ref.shape[0]
    n_heads = n_kv * kv_group
    rows = n_heads * t_len
    grows = kv_group * t_len
    nblk = ps_ref.shape[0]
    lane = lax.broadcasted_iota(jnp.int32, (rows, ROUTE_LANES), 1)
    m_blk = jnp.full((rows, ROUTE_LANES), NEG, F32)
    l_blk = jnp.zeros((rows, ROUTE_LANES), F32)
    gates = jnp.full((rows, ROUTE_LANES), NEG, F32)
    for n in range(nblk):
        st = ps_ref[n]
        m_blk = jnp.where(lane == n, st[:, 0:1], m_blk)
        l_blk = jnp.where(lane == n, st[:, 1:2], l_blk)
        gates = jnp.where(lane == n, st[:, 2:3], gates)
    w_blk = jnp.zeros((rows, ROUTE_LANES), F32)
    for _ in range(min(MOBA_TOPK, nblk)):
        best = jnp.max(gates, axis=-1, keepdims=True)
        first = jnp.min(jnp.where(gates == best, lane, ROUTE_LANES), axis=-1, keepdims=True)
        w_blk = jnp.where(lane == first, 1.0, w_blk)
        gates = jnp.where(lane == first, NEG, gates)
    m_blk = jnp.where(w_blk > 0.0, m_blk, NEG)
    m_past = jnp.max(m_blk, axis=-1, keepdims=True)

    qn = _sample_queries(q_ref[...], gq_ref[...], n_heads, hd)
    slope = _row_slopes(slopes_ref, n_heads, t_len)
    row = lax.broadcasted_iota(jnp.int32, (rows, 1), 0)
    tq = row % t_len
    tk = lax.broadcasted_iota(jnp.int32, (1, t_len), 1)
    s_own = jnp.concatenate([
        lax.dot_general(qn[g * grows:(g + 1) * grows].astype(BF16), kn_ref[:, g * hd:(g + 1) * hd].astype(BF16), _NT,
                        preferred_element_type=F32) for g in range(n_kv)], axis=0) * (hd ** -0.5)
    s_own = jnp.where(tk <= tq, s_own - slope * (tq - tk).astype(F32), NEG)
    m_all = jnp.maximum(m_past, jnp.max(s_own, axis=-1, keepdims=True))
    w_blk = jnp.where(w_blk > 0.0, jnp.exp(m_blk - m_all), 0.0)
    p_own = jnp.exp(s_own - m_all)
    den = jnp.sum(p_own, axis=-1, keepdims=True) + jnp.sum(w_blk * l_blk, axis=-1, keepdims=True)
    num = jnp.zeros((rows, hd), F32)
    for n in range(nblk):
        num = num + w_blk[:, n:n + 1] * po_ref[n]
    num = num + jnp.concatenate([
        jnp.dot(p_own[g * grows:(g + 1) * grows].astype(BF16), vn_ref[:, g * hd:(g + 1) * hd].astype(BF16),
                preferred_element_type=F32) for g in range(n_kv)], axis=0)
    out = num / den
    for h in range(n_heads):
        o_ref[:, h * hd:(h + 1) * hd] = out[h * t_len:(h + 1) * t_len]


def _moba_merge(z, gq, kv, part_o, part_s, slopes, *, n_batch, t_len, row_off, n_kv, kv_group, alias):
    hd = gq.shape[1]
    kw = n_kv * hd
    nblk, rows = part_o.shape[1:3]
    body = functools.partial(_moba_merge_body, n_kv=n_kv, kv_group=kv_group)
    wrapped = lambda *refs: body(*refs[:7], *refs[8:])
    new_rows = lambda b, s: (row_off // t_len + b, 0)
    grid_spec = pltpu.PrefetchScalarGridSpec(
        num_scalar_prefetch=1,
        grid=(n_batch,),
        in_specs=[
            pl.BlockSpec((t_len, kv_group * kw), new_rows),
            pl.BlockSpec((1, hd), lambda b, s: (0, 0)),
            pl.BlockSpec((t_len, kw), new_rows),
            pl.BlockSpec((t_len, kw), lambda b, s: (row_off // t_len + b, 1)),
            pl.BlockSpec((None, nblk, rows, hd), lambda b, s: (b, 0, 0, 0)),
            pl.BlockSpec((None, nblk, rows, ROUTE_LANES), lambda b, s: (b, 0, 0, 0)),
            pl.BlockSpec(memory_space=pl.ANY),
        ],
        out_specs=pl.BlockSpec((t_len, kv_group * kw), new_rows),
    )
    return pl.pallas_call(
        wrapped,
        out_shape=jax.ShapeDtypeStruct(alias.shape, F32),
        grid_spec=grid_spec,
        input_output_aliases={7: 0},
        compiler_params=_params("parallel"),
        name="moba_merge",
    )(slopes, z, gq, kv, kv, part_o, part_s, alias)


def _tile(m, pref):
    return pref if m >= pref else m


def kernel(x_prompt, x_sample, mem_prompt, cache_k, cache_v, cache_mem_k, cache_mem_v, state_pool, page_table, g_mix_norm, w_in, w_out, w_pool, pool_scale, g_q, g_kv_norm, w_kv, g_k, g_mem_norm, w_mem_kv, g_mem_q, g_mem_k, g_ffn_norm, w_group, b_group, w_router, b_router, w_gate_up, w_down):
    bp, tp, d = x_prompt.shape
    bs, ts, _ = x_sample.shape
    depth = w_in.shape[0]
    n_a = w_pool.shape[0]
    mem_tokens = mem_prompt.shape[1]
    mem_w = d // 4
    mem_hd = mem_w // MEM_HEADS
    tok_w = d - mem_w
    page, n_kv, hd = cache_k.shape[1:]
    kv_w = n_kv * hd
    n_heads = tok_w // hd
    kv_group = n_heads // n_kv
    past_len = page_table.shape[1] * page
    n_groups = w_group.shape[2]
    n_experts = w_router.shape[2]
    per_group = n_experts // n_groups
    d_expert = w_down.shape[2]
    mp, ms = bp * tp, bs * ts
    m_all = mp + ms

    tm_p = _tile(mp, 512)
    tm_d = _tile(mp, BIG_ROW_TILE)
    tn = _tile(d, 1024)
    tn_s = _tile(d, 512)
    tn_o = _tile(d, 512)
    tt_p = _tile(tp, MOBA_BLOCK)
    tn_e = _tile(d, 2048)
    tm_e = 1024 if m_all >= 4096 else 128
    sub_e = tm_e // 4
    tm_c = _tile(mp, 256)

    row = lambda v: v.reshape(1, -1).astype(F32)
    w_in_b, w_out_b, w_pool_b = w_in.astype(BF16), w_out.astype(BF16), w_pool.astype(BF16)
    w_kv_b, w_mem_kv_b = w_kv.astype(BF16)[None], w_mem_kv.astype(BF16)
    w_rt = jnp.concatenate([w_group, w_router, jnp.zeros((depth, d, ROUTE_LANES - n_groups - n_experts), F32)], axis=-1)
    w_rt_hi = w_rt.astype(BF16)
    w_rt_lo = (w_rt - w_rt_hi.astype(F32)).astype(BF16)
    b_rt = jnp.concatenate([b_group, b_router, jnp.zeros((depth, ROUTE_LANES - n_groups - n_experts), F32)],
                           axis=-1)[:, None, :]
    slopes = jnp.exp2(-8.0 * jnp.arange(1, n_heads + 1, dtype=F32) / n_heads)

    xp = x_prompt.reshape(mp, d)
    xs_ = x_sample.reshape(ms, d)
    sample_blk = mp // ms

    memx = mem_prompt.reshape(bp * mem_tokens, d)
    mem_kv_p = [
        _norm_matmul(memx, row(g_mem_norm[l]), w_mem_kv_b, layer=l, n_out=2 * mem_w, tm=_tile(bp * mem_tokens, 512),
                     tn=mem_w, head_groups=(mem_hd, 0),
                     gh=jnp.concatenate([jnp.tile(row(g_mem_k[l]), (1, MEM_HEADS)), jnp.ones((1, mem_w), F32)], axis=1))
        for l in range(depth)]
    mem_kv_p3 = [a.reshape(bp, mem_tokens, 2 * mem_w) for a in mem_kv_p]
    cmk = cache_mem_k.reshape(depth, bs, mem_tokens, mem_w)
    cmv = cache_mem_v.reshape(depth, bs, mem_tokens, mem_w)

    def mem_attend(z, l):
        y = _mem_attn(z, row(g_mem_q[l]),
                      mem_kv_p3[l], pl.BlockSpec((None, mem_tokens, mem_w), lambda b, t: (b, 0, 0)),
                      mem_kv_p3[l], pl.BlockSpec((None, mem_tokens, mem_w), lambda b, t: (b, 0, 1)),
                      n_batch=bp, t_len=tp, tt=tt_p, row_off=0, out_rows=m_all)
        return _mem_attn(z, row(g_mem_q[l]),
                         cmk, pl.BlockSpec((None, None, mem_tokens, mem_w), lambda b, t: (l, b, 0, 0)),
                         cmv, pl.BlockSpec((None, None, mem_tokens, mem_w), lambda b, t: (l, b, 0, 0)),
                         n_batch=bs, t_len=ts, tt=ts, row_off=mp, out_rows=m_all, alias=y)

    def moe_experts(x, l):
        return _moe_experts(x, row(g_ffn_norm[l]), w_rt_hi, w_rt_lo, b_rt, w_gate_up, w_down, layer=l,
                            n_groups=n_groups, per_group=per_group, tm_route=tm_p, tm=tm_e, sub=sub_e, tn=tn_e,
                            tm_c=tm_c)

    x = None
    z_pool = []
    kv = None
    y_prompt = y_sample = None
    for l in range(depth):
        g_mix = row(g_mix_norm[l])
        if l == 0:
            z = _norm_matmul(xp, g_mix, w_in_b, layer=l, n_out=d, tm=tm_d, tn=tn_o, out_rows=m_all)
            z = _norm_matmul(xs_, g_mix, w_in, layer=l, n_out=d, tm=ms, tn=tn_s, out_rows=m_all,
                             row_blk_off=sample_blk, alias=z)
        else:
            z = _norm_matmul(x, g_mix, w_in_b, layer=l, n_out=d, n_rows=mp, tm=tm_d, tn=tn_o, out_rows=m_all)
            z = _norm_matmul(x, g_mix, w_in_b, layer=l, n_out=d, n_rows=ms, tm=ms, tn=tn, in_blk_off=sample_blk,
                             out_rows=m_all, row_blk_off=sample_blk, alias=z)
        mem_y = mem_attend(z, l)
        if l < n_a:
            z_pool.append(z)
            scale = row(pool_scale[l])
            zero_buf = jnp.zeros((bp, POOL_HALO, tok_w), F32)
            samp_buf = jnp.concatenate([jnp.zeros((bs, 1, tok_w), F32), state_pool[l]], axis=1)
            tok_y = _pool_mix(z, zero_buf, w_pool_b[l], scale, n_batch=bp, t_len=tp, tt=tt_p, row_off=0, pos0=0,
                              out_rows=m_all)
            tok_y = _pool_mix(z, samp_buf, w_pool[l], scale, n_batch=bs, t_len=ts, tt=ts, row_off=mp,
                              pos0=past_len, out_rows=m_all, alias=tok_y)
        else:
            if kv is None:
                gh_kv = jnp.concatenate([jnp.tile(row(g_k), (1, n_kv)), jnp.ones((1, kv_w), F32)], axis=1)
                kv = _norm_matmul(x, row(g_kv_norm), w_kv_b, layer=0, n_out=2 * kv_w, n_rows=mp, tm=tm_d,
                                  tn=kv_w // 2, head_groups=(hd, hd, 0, 0), gh=gh_kv, out_rows=m_all)
                kv = _norm_matmul(x, row(g_kv_norm), w_kv_b, layer=0, n_out=2 * kv_w, n_rows=ms, tm=ms, tn=kv_w,
                                  head_groups=(hd, 0), gh=gh_kv, in_blk_off=sample_blk, out_rows=m_all,
                                  row_blk_off=sample_blk, alias=kv)
            gq = row(g_q[l - n_a])
            tok_y = _moba_prompt(z, gq, kv, slopes, n_batch=bp, t_len=tp, n_kv=n_kv, kv_group=kv_group,
                                 out_rows=m_all)
            part_o, part_s = _moba_partials(z, gq, cache_k, cache_v, page_table, slopes, n_batch=bs, t_len=ts,
                                            row_off=mp, kv_group=kv_group, past_len=past_len)
            tok_y = _moba_merge(z, gq, kv, part_o, part_s, slopes, n_batch=bs, t_len=ts, row_off=mp,
                                n_kv=n_kv, kv_group=kv_group, alias=tok_y)
        if l == 0:
            x = _out_proj(tok_y, mem_y, w_out_b, xp, layer=l, tm=tm_d, tn=tn_o, out_rows=m_all)
            x = _out_proj(tok_y, mem_y, w_out, xs_, layer=l, tm=ms, tn=tn_s, in_blk_off=sample_blk, out_rows=m_all,
                          alias=x)
        else:
            x_att = _out_proj(tok_y, mem_y, w_out_b, x, layer=l, n_rows=mp, tm=tm_d, tn=tn_o, out_rows=m_all)
            x = _out_proj(tok_y, mem_y, w_out_b, x, layer=l, n_rows=ms, tm=ms, tn=tn, in_blk_off=sample_blk,
                          x_blk_off=sample_blk, out_rows=m_all, alias=x_att)
        info, ys, pair_slot = moe_experts(x, l)
        if l + 1 < depth:
            x = _combine(x, info, ys, pair_slot, tm=tm_c)
        else:
            y_prompt = _combine(x, info, ys, pair_slot, tm=tm_c, n_rows=mp)
            y_sample = _combine(x, info, ys, pair_slot, tm=ms, row_blk_off=sample_blk, n_rows=ms)

    y_prompt = y_prompt.reshape(bp, tp, d)
    y_sample = y_sample.reshape(bs, ts, d)
    k_prompt = kv[:mp, :kv_w].reshape(bp, tp, n_kv, hd)
    v_prompt = kv[:mp, kv_w:].reshape(bp, tp, n_kv, hd)
    k_sample = kv[mp:, :kv_w].reshape(bs, ts, n_kv, hd)
    v_sample = kv[mp:, kv_w:].reshape(bs, ts, n_kv, hd)
    mem_k_prompt = jnp.stack([a[:, :, :mem_w].reshape(bp, mem_tokens, MEM_HEADS, mem_hd) for a in mem_kv_p3])
    mem_v_prompt = jnp.stack([a[:, :, mem_w:].reshape(bp, mem_tokens, MEM_HEADS, mem_hd) for a in mem_kv_p3])
    pool_prompt = jnp.stack([jnp.stack([zz[(b + 1) * tp - POOL_BUF:(b + 1) * tp, :tok_w] for b in range(bp)])
                             for zz in z_pool])
    pool_sample = jnp.stack([
        jnp.concatenate([state_pool[i], zz[mp:].reshape(bs, ts, d)[:, :, :tok_w]], axis=1)[:, -POOL_BUF:]
        for i, zz in enumerate(z_pool)])
    return (y_prompt, y_sample, k_prompt, v_prompt, k_sample, v_sample,
            mem_k_prompt, mem_v_prompt, pool_prompt, pool_sample)
```
